```python
import jax, jax.numpy as jnp
from jax import lax
import numpy as np

D_MODEL = 4096
BATCH = 1
SEQ = 16384
DEPTH = 1

D_RNN = D_MODEL
N_RNN_BLOCKS = 16
RNN_BLOCK = D_RNN // N_RNN_BLOCKS
CONV_WIDTH = 4
LRU_C = 8.0
HEAD_DIM = 64
N_Q_HEADS = 64
N_KV_HEADS = 8
GQA_GROUP = N_Q_HEADS // N_KV_HEADS
D_ATTN = N_Q_HEADS * HEAD_DIM
D_KV = N_KV_HEADS * HEAD_DIM
WINDOW = 128
BLOCK_Q = 128
EPS = 1e-6

SPLIT_SIZES = (D_RNN, D_RNN, D_ATTN, D_KV, D_KV, D_ATTN, D_MODEL, D_MODEL)
SPLIT_POINTS = (D_RNN,
                2 * D_RNN,
                2 * D_RNN + D_ATTN,
                2 * D_RNN + D_ATTN + D_KV,
                2 * D_RNN + D_ATTN + 2 * D_KV,
                2 * D_RNN + 2 * D_ATTN + 2 * D_KV,
                2 * D_RNN + 2 * D_ATTN + 2 * D_KV + D_MODEL)
D_IN = 2 * D_RNN + 2 * D_ATTN + 2 * D_KV + 2 * D_MODEL

kernel_name = "hybrid_rglru_swa_sink_adaln_block"


def rmsnorm(x, gain):
    xf = x.astype(jnp.float32)
    return xf * lax.rsqrt(jnp.mean(xf * xf, axis=-1, keepdims=True) + EPS) * gain.astype(jnp.float32)


def causal_depthwise_conv(u, w, b):
    s = u.shape[1]
    up = jnp.pad(u, ((0, 0), (CONV_WIDTH - 1, 0), (0, 0)))
    out = b
    for k in range(CONV_WIDTH):
        out = out + w[k] * up[:, k:k + s]
    return out


def rg_lru(u, w_a, b_a, w_x, b_x, lam):
    bsz, s, _ = u.shape
    uf = u.astype(jnp.float32)
    ub = uf.reshape(bsz, s, N_RNN_BLOCKS, RNN_BLOCK)
    r = jax.nn.sigmoid(jnp.einsum('bshi,hij->bshj', ub, w_a.astype(jnp.float32)) + b_a).reshape(bsz, s, D_RNN)
    i = jax.nn.sigmoid(jnp.einsum('bshi,hij->bshj', ub, w_x.astype(jnp.float32)) + b_x).reshape(bsz, s, D_RNN)
    log_a = -LRU_C * r * jax.nn.softplus(-lam.astype(jnp.float32))
    a = jnp.exp(log_a)
    b_in = jnp.sqrt(-jnp.expm1(2.0 * log_a)) * (i * uf)

    def combine(left, right):
        a1, b1 = left
        a2, b2 = right
        return a1 * a2, a2 * b1 + b2

    _, h = lax.associative_scan(combine, (a, b_in), axis=1)
    return h.astype(u.dtype)


def sliding_window_attention_with_sinks(q, k, v, sinks):
    bsz, s, _ = q.shape
    nb = s // BLOCK_Q
    qb = q.reshape(bsz, nb, BLOCK_Q, N_KV_HEADS, GQA_GROUP, HEAD_DIM)
    kb = k.reshape(bsz, nb, BLOCK_Q, N_KV_HEADS, HEAD_DIM)
    vb = v.reshape(bsz, nb, BLOCK_Q, N_KV_HEADS, HEAD_DIM)
    pad = jnp.zeros_like(kb[:, :1])
    k_band = jnp.concatenate([jnp.concatenate([pad, kb[:, :-1]], axis=1), kb], axis=2)
    v_band = jnp.concatenate([jnp.concatenate([pad, vb[:, :-1]], axis=1), vb], axis=2)
    scale = HEAD_DIM ** -0.5
    scores = jnp.einsum('bnqhgd,bnkhd->bnhgqk', qb, k_band).astype(jnp.float32) * scale
    qi = jnp.arange(BLOCK_Q)[:, None]
    kj = jnp.arange(2 * BLOCK_Q)[None, :]
    dist = qi + BLOCK_Q - kj
    in_window = (dist >= 0) & (dist < WINDOW)
    first_block = (jnp.arange(nb) == 0)[:, None, None]
    valid = in_window[None] & ~(first_block & (kj < BLOCK_Q)[None])
    scores = jnp.where(valid[None, :, None, None], scores, -jnp.inf)
    sink = sinks.astype(jnp.float32).reshape(N_KV_HEADS, GQA_GROUP)[None, None, :, :, None, None]
    m = jnp.maximum(jnp.max(scores, axis=-1, keepdims=True), sink)
    p = jnp.exp(scores - m)
    denom = jnp.sum(p, axis=-1, keepdims=True) + jnp.exp(sink - m)
    probs = (p / denom).astype(v.dtype)
    o = jnp.einsum('bnhgqk,bnkhd->bnqhgd', probs, v_band)
    return o.reshape(bsz, s, D_ATTN)


def setup_inputs(seed: int = 0) -> dict:
    key = jax.random.key(seed)
    ks = jax.random.split(key, 20)
    f32 = jnp.float32
    L = DEPTH
    x = jax.random.normal(ks[0], (BATCH, SEQ, D_MODEL), f32)
    c = jax.random.normal(ks[1], (BATCH, D_MODEL), f32)
    w_ada = jax.random.normal(ks[2], (L, D_MODEL, 3 * D_MODEL), f32) * (0.5 * D_MODEL ** -0.5)
    b_ada = 0.01 * jax.random.normal(ks[3], (L, 3 * D_MODEL), f32)
    norm_gain = 1.0 + 0.02 * jax.random.normal(ks[4], (L, D_MODEL), f32)
    w_in = jax.random.normal(ks[5], (L, D_MODEL, D_IN), f32) * D_MODEL ** -0.5
    conv_w = jax.random.normal(ks[6], (L, CONV_WIDTH, D_RNN), f32) * CONV_WIDTH ** -0.5
    conv_b = 0.01 * jax.random.normal(ks[7], (L, D_RNN), f32)
    w_gate_a = jax.random.normal(ks[8], (L, N_RNN_BLOCKS, RNN_BLOCK, RNN_BLOCK), f32) * RNN_BLOCK ** -0.5
    b_gate_a = 0.01 * jax.random.normal(ks[9], (L, N_RNN_BLOCKS, RNN_BLOCK), f32)
    w_gate_x = jax.random.normal(ks[10], (L, N_RNN_BLOCKS, RNN_BLOCK, RNN_BLOCK), f32) * RNN_BLOCK ** -0.5
    b_gate_x = 0.01 * jax.random.normal(ks[11], (L, N_RNN_BLOCKS, RNN_BLOCK), f32)
    a0 = jax.random.uniform(ks[12], (L, D_RNN), f32, minval=0.9, maxval=0.999)
    lru_lambda = jnp.log(a0) - jnp.log1p(-a0)
    attn_sinks = 0.5 * jax.random.normal(ks[13], (L, N_Q_HEADS), f32)
    w_rnn_proj = jax.random.normal(ks[14], (L, D_RNN, D_MODEL), f32) * D_RNN ** -0.5
    w_attn_proj = jax.random.normal(ks[15], (L, D_ATTN, D_MODEL), f32) * D_ATTN ** -0.5
    w_out = jax.random.normal(ks[16], (L, D_MODEL, D_MODEL), f32) * D_MODEL ** -0.5
    final_norm_gain = 1.0 + 0.02 * jax.random.normal(ks[17], (D_MODEL,), f32)
    return {"x": x, "c": c, "w_ada": w_ada, "b_ada": b_ada, "norm_gain": norm_gain,
            "w_in": w_in, "conv_w": conv_w, "conv_b": conv_b,
            "w_gate_a": w_gate_a, "b_gate_a": b_gate_a, "w_gate_x": w_gate_x, "b_gate_x": b_gate_x,
            "lru_lambda": lru_lambda, "attn_sinks": attn_sinks,
            "w_rnn_proj": w_rnn_proj, "w_attn_proj": w_attn_proj, "w_out": w_out,
            "final_norm_gain": final_norm_gain}


def reference(x, c, w_ada, b_ada, norm_gain, w_in, conv_w, conv_b, w_gate_a, b_gate_a,
              w_gate_x, b_gate_x, lru_lambda, attn_sinks, w_rnn_proj, w_attn_proj, w_out,
              final_norm_gain):
    dt = x.dtype
    c_act = jax.nn.silu(c)
    for l in range(DEPTH):
        mod = jnp.einsum('bd,de->be', c_act, w_ada[l]) + b_ada[l]
        shift, scale, gate = jnp.split(mod, 3, axis=-1)
        h = rmsnorm(x, norm_gain[l]) * (1.0 + scale[:, None, :].astype(jnp.float32)) + shift[:, None, :].astype(jnp.float32)
        h = h.astype(dt)
        proj = jnp.einsum('bsd,de->bse', h, w_in[l])
        u_rnn, g_rnn, q, k, v, g_attn, m_rnn, m_attn = jnp.split(proj, SPLIT_POINTS, axis=-1)
        u = causal_depthwise_conv(u_rnn, conv_w[l], conv_b[l])
        y_rnn = rg_lru(u, w_gate_a[l], b_gate_a[l], w_gate_x[l], b_gate_x[l], lru_lambda[l]) * jax.nn.silu(g_rnn)
        y_attn = sliding_window_attention_with_sinks(q, k, v, attn_sinks[l]) * jax.nn.silu(g_attn)
        merged = (jax.nn.sigmoid(m_rnn) * jnp.einsum('bse,ed->bsd', y_rnn, w_rnn_proj[l])
                  + jax.nn.sigmoid(m_attn) * jnp.einsum('bse,ed->bsd', y_attn, w_attn_proj[l]))
        out = jnp.einsum('bsd,de->bse', merged, w_out[l])
        x = x + gate[:, None, :] * out
    return rmsnorm(x, final_norm_gain).astype(dt)
```

```python
import functools

import jax
import jax.numpy as jnp
from jax import lax
from jax.experimental import pallas as pl
from jax.experimental.pallas import tpu as pltpu

EPS = 1e-6
LRU_C = 8.0
WINDOW = 128
BLOCK_Q = 128

V7X_VMEM_BYTES = 64 * 1024 * 1024
VMEM_HEADROOM_BYTES = 6 * 1024 * 1024
LANES = 128
SUBLANES = 8


def _vmem_limit(block_bytes):
    need = int(block_bytes) + VMEM_HEADROOM_BYTES
    assert need <= V7X_VMEM_BYTES - 2 * 1024 * 1024, need
    return need


def _sigmoid(v):
    return 0.5 * jnp.tanh(0.5 * v) + 0.5


def _silu(v):
    return v * _sigmoid(v)


def _mod_kernel(c_ref, w_ref, b_ref, o_ref):
    c_act = _silu(c_ref[...])
    lhs = jnp.broadcast_to(c_act, (SUBLANES, c_act.shape[1]))
    acc = jnp.dot(lhs, w_ref[...], preferred_element_type=jnp.float32,
                  precision=lax.Precision.HIGHEST)
    o_ref[...] = acc[0:1, :] + b_ref[...]


def _modulation(c, w_ada, b_ada):
    d, n = w_ada.shape
    tn = 512
    return pl.pallas_call(
        _mod_kernel,
        out_shape=jax.ShapeDtypeStruct((1, n), jnp.float32),
        grid=(n // tn,),
        in_specs=[pl.BlockSpec((1, d), lambda j: (0, 0)),
                  pl.BlockSpec((d, tn), lambda j: (0, j)),
                  pl.BlockSpec((1, tn), lambda j: (0, j))],
        out_specs=pl.BlockSpec((1, tn), lambda j: (0, j)),
        compiler_params=pltpu.CompilerParams(
            dimension_semantics=("arbitrary",),
            vmem_limit_bytes=_vmem_limit(2 * d * tn * 4)),
        name="adaln_mod",
    )(c, w_ada, b_ada.reshape(1, n))


def _norm_mod_kernel(x_ref, gain_ref, scale_ref, shift_ref, o_ref):
    x = x_ref[...]
    ms = jnp.mean(x * x, axis=-1, keepdims=True)
    g = gain_ref[...] * (1.0 + scale_ref[...])
    o_ref[...] = (x * lax.rsqrt(ms + EPS) * g + shift_ref[...]).astype(o_ref.dtype)


def _norm_mod(x2, gain, scale, shift):
    s, d = x2.shape
    tm = 256
    row = pl.BlockSpec((1, d), lambda i: (0, 0))
    return pl.pallas_call(
        _norm_mod_kernel,
        out_shape=jax.ShapeDtypeStruct((s, d), jnp.bfloat16),
        grid=(s // tm,),
        in_specs=[pl.BlockSpec((tm, d), lambda i: (i, 0)), row, row, row],
        out_specs=pl.BlockSpec((tm, d), lambda i: (i, 0)),
        compiler_params=pltpu.CompilerParams(
            dimension_semantics=("arbitrary",),
            vmem_limit_bytes=_vmem_limit(2 * tm * d * (4 + 2) + 2 * tm * d * 4)),
        name="norm_mod",
    )(x2, gain, scale, shift)


def _final_norm_kernel(x_ref, gain_ref, o_ref):
    x = x_ref[...]
    ms = jnp.mean(x * x, axis=-1, keepdims=True)
    o_ref[...] = x * lax.rsqrt(ms + EPS) * gain_ref[...]


def _final_norm(x2, gain):
    s, d = x2.shape
    tm = 256
    return pl.pallas_call(
        _final_norm_kernel,
        out_shape=jax.ShapeDtypeStruct((s, d), jnp.float32),
        grid=(s // tm,),
        in_specs=[pl.BlockSpec((tm, d), lambda i: (i, 0)),
                  pl.BlockSpec((1, d), lambda i: (0, 0))],
        out_specs=pl.BlockSpec((tm, d), lambda i: (i, 0)),
        compiler_params=pltpu.CompilerParams(
            dimension_semantics=("arbitrary",),
            vmem_limit_bytes=_vmem_limit(2 * tm * d * (4 + 4) + 2 * tm * d * 4)),
        name="final_norm",
    )(x2, gain)


def _ep_identity(acc):
    return acc


def _ep_silu(acc):
    return _silu(acc)


def _ep_sigmoid(acc):
    return _sigmoid(acc)


def _ep_scale(acc, a):
    return a * acc


def _ep_scale_add(acc, a, b):
    return a * acc + b


def _ep_residual(acc, x, gate):
    return x + gate * acc


def _matmul_kernel(*refs, epilogue, n_tiles, n_rows):
    lhs_ref, rhs_ref = refs[0], refs[1]
    tile_refs = refs[2:2 + n_tiles]
    row_refs = refs[2 + n_tiles:2 + n_tiles + n_rows]
    o_ref = refs[-1]
    acc = jnp.dot(lhs_ref[...], rhs_ref[...], preferred_element_type=jnp.float32)
    extras = [r[...] for r in tile_refs] + [r[...] for r in row_refs]
    o_ref[...] = epilogue(acc, *extras).astype(o_ref.dtype)


def _matmul(lhs, rhs, *, col_offset, n_cols, out_dtype, epilogue=_ep_identity,
            tiles=(), rows=(), tn=1024, name):
    m, k = lhs.shape
    tm = 512 if tiles else 1024
    assert rhs.shape[0] == k and col_offset % tn == 0 and n_cols % tn == 0 and m % tm == 0
    off = col_offset // tn
    in_specs = [pl.BlockSpec((tm, k), lambda i, j: (i, 0)),
                pl.BlockSpec((k, tn), lambda i, j: (0, off + j))]
    in_specs += [pl.BlockSpec((tm, tn), lambda i, j: (i, j)) for _ in tiles]
    in_specs += [pl.BlockSpec((1, tn), lambda i, j: (0, j)) for _ in rows]
    out_bytes = jnp.dtype(out_dtype).itemsize
    tile_bytes = sum(t.dtype.itemsize for t in tiles)
    block_bytes = 2 * (tm * k * lhs.dtype.itemsize + k * tn * rhs.dtype.itemsize
                       + tm * tn * (out_bytes + tile_bytes)) + 2 * tm * tn * 4
    return pl.pallas_call(
        functools.partial(_matmul_kernel, epilogue=epilogue, n_tiles=len(tiles),
                          n_rows=len(rows)),
        out_shape=jax.ShapeDtypeStruct((m, n_cols), out_dtype),
        grid=(m // tm, n_cols // tn),
        in_specs=in_specs,
        out_specs=pl.BlockSpec((tm, tn), lambda i, j: (i, j)),
        compiler_params=pltpu.CompilerParams(
            dimension_semantics=("arbitrary", "arbitrary"),
            vmem_limit_bytes=_vmem_limit(block_bytes)),
        name=name,
    )(lhs, rhs, *tiles, *rows)


RNN_SUB_ROWS = 256
RNN_BLOCK_ROWS = 2048


def _rglru_kernel(u_ref, sg_ref, cw_ref, cb_ref, wa_ref, ba_ref, wx_ref, bx_ref, lam_ref,
                  y_ref, ubuf, carry_ref, *, conv_width):
    ts = RNN_SUB_ROWS
    halo = SUBLANES
    width = u_ref.shape[1]

    @pl.when(pl.program_id(1) == 0)
    def _():
        ubuf[ts:ts + halo, :] = jnp.zeros((halo, width), jnp.float32)
        carry_ref[...] = jnp.zeros_like(carry_ref)

    lam = lam_ref[...]
    neg = -lam
    softplus_neg = jnp.maximum(neg, 0.0) + jnp.log1p(jnp.exp(-jnp.abs(neg)))
    decay = -LRU_C * softplus_neg
    cw = cw_ref[...]
    row_in_group = lax.broadcasted_iota(jnp.int32, (ts // SUBLANES, SUBLANES, width), 1)

    def sub_tile(s, _):
        r0 = pl.multiple_of(s * ts, ts)
        ubuf[0:halo, :] = ubuf[ts:ts + halo, :]
        ubuf[halo:ts + halo, :] = u_ref[pl.ds(r0, ts), :]
        u = cb_ref[...]
        for k in range(conv_width):
            u = u + cw[k:k + 1, :] * ubuf[pl.ds(halo - (conv_width - 1) + k, ts), :]
        ub = u.astype(jnp.bfloat16)
        r = _sigmoid(jnp.dot(ub, wa_ref[...], preferred_element_type=jnp.float32) + ba_ref[...])
        i = _sigmoid(jnp.dot(ub, wx_ref[...], preferred_element_type=jnp.float32) + bx_ref[...])
        log_a = decay * r
        a = jnp.exp(log_a)
        b = jnp.sqrt(1.0 - a * a) * (i * u)

        a3 = a.reshape(ts // SUBLANES, SUBLANES, width)
        b3 = b.reshape(ts // SUBLANES, SUBLANES, width)
        shift = 1
        while shift < SUBLANES:
            keep = row_in_group >= shift
            a_prev = jnp.where(keep, pltpu.roll(a3, shift, axis=1), 1.0)
            b_prev = jnp.where(keep, pltpu.roll(b3, shift, axis=1), 0.0)
            b3 = a3 * b_prev + b3
            a3 = a3 * a_prev
            shift *= 2

        carry = carry_ref[...]
        for g in range(ts // SUBLANES):
            hg = b3[g] + a3[g] * carry
            carry = hg[SUBLANES - 1:SUBLANES, :]
            rows = pl.ds(pl.multiple_of(r0 + g * SUBLANES, SUBLANES), SUBLANES)
            y_ref[rows, :] = (hg * sg_ref[rows, :]).astype(y_ref.dtype)
        carry_ref[...] = carry
        return 0

    lax.fori_loop(0, u_ref.shape[0] // ts, sub_tile, 0)


def _rglru(u_rnn, sg_rnn, conv_w, conv_b, w_gate_a, b_gate_a, w_gate_x, b_gate_x, lam):
    s, d = u_rnn.shape
    nb, bw, _ = w_gate_a.shape
    conv_width = conv_w.shape[0]
    tb = RNN_BLOCK_ROWS
    seq = pl.BlockSpec((tb, bw), lambda c, t: (t, c))
    vec = pl.BlockSpec((1, bw), lambda c, t: (0, c))
    gate_w = pl.BlockSpec((None, bw, bw), lambda c, t: (c, 0, 0))
    gate_b = pl.BlockSpec((None, 1, bw), lambda c, t: (c, 0, 0))
    block_bytes = 2 * tb * bw * (4 + 4 + 2) + 4 * bw * bw * 2 + (RNN_SUB_ROWS + 16) * bw * 4
    return pl.pallas_call(
        functools.partial(_rglru_kernel, conv_width=conv_width),
        out_shape=jax.ShapeDtypeStruct((s, d), jnp.bfloat16),
        grid=(nb, s // tb),
        in_specs=[seq, seq,
                  pl.BlockSpec((conv_width, bw), lambda c, t: (0, c)), vec,
                  gate_w, gate_b, gate_w, gate_b, vec],
        out_specs=seq,
        scratch_shapes=[pltpu.VMEM((RNN_SUB_ROWS + SUBLANES, bw), jnp.float32),
                        pltpu.VMEM((1, bw), jnp.float32)],
        compiler_params=pltpu.CompilerParams(
            dimension_semantics=("arbitrary", "arbitrary"),
            vmem_limit_bytes=_vmem_limit(block_bytes + 16 * 1024 * 1024)),
        name="rglru",
    )(u_rnn, sg_rnn, conv_w, conv_b.reshape(1, d),
      w_gate_a.astype(jnp.bfloat16), b_gate_a.reshape(nb, 1, bw),
      w_gate_x.astype(jnp.bfloat16), b_gate_x.reshape(nb, 1, bw), lam.reshape(1, d))


def _swap_halves(v):
    half = v.shape[1] // 2
    return jnp.concatenate([v[:, half:], v[:, :half]], axis=1)


def _attn_kernel(sink_ref, q_ref, kv_ref, kvp_ref, sg_ref, o_ref, *, n_kv, group, head_dim):
    bq = q_ref.shape[0]
    pair_w = 2 * head_dim
    pairs_per_group = group // 2
    d_kv = n_kv * head_dim
    n = pl.program_id(0)

    q_pos = lax.broadcasted_iota(jnp.int32, (bq, 2 * bq), 0)
    k_pos = lax.broadcasted_iota(jnp.int32, (bq, 2 * bq), 1)
    dist = q_pos + bq - k_pos
    valid = (dist >= 0) & (dist < WINDOW) & ((k_pos >= bq) | (n > 0))
    lane = lax.broadcasted_iota(jnp.int32, (bq, pair_w), 1)
    lo_q = lane < head_dim
    lane_kv = lax.broadcasted_iota(jnp.int32, (2 * bq, pair_w), 1)
    lo_kv = lane_kv < head_dim
    scale = head_dim ** -0.5
    zero_q = jnp.zeros((bq, pair_w), q_ref.dtype)

    for kv_pair in range(n_kv // 2):
        cols = slice(kv_pair * pair_w, (kv_pair + 1) * pair_w)
        k_two = jnp.concatenate([kvp_ref[:, cols], kv_ref[:, cols]], axis=0)
        v_cols = slice(d_kv + kv_pair * pair_w, d_kv + (kv_pair + 1) * pair_w)
        v_two = jnp.concatenate([kvp_ref[:, v_cols], kv_ref[:, v_cols]], axis=0)
        k_swapped = _swap_halves(k_two)
        v_swapped = _swap_halves(v_two)
        for parity in range(2):
            hk = 2 * kv_pair + parity
            if parity == 0:
                k_dup = jnp.where(lo_kv, k_two, k_swapped)
                v_dup = jnp.where(lo_kv, v_two, v_swapped)
            else:
                k_dup = jnp.where(lo_kv, k_swapped, k_two)
                v_dup = jnp.where(lo_kv, v_swapped, v_two)
            k_dup = k_dup * jnp.asarray(scale, k_dup.dtype)
            q_rows = []
            for p in range(pairs_per_group):
                pair = hk * pairs_per_group + p
                qp = q_ref[:, pair * pair_w:(pair + 1) * pair_w]
                q_rows.append(jnp.where(lo_q, qp, zero_q))
                q_rows.append(jnp.where(lo_q, zero_q, qp))
            q_stack = jnp.concatenate(q_rows, axis=0)
            s_all = lax.dot_general(q_stack, k_dup, (((1,), (1,)), ((), ())),
                                    preferred_element_type=jnp.float32)
            p_rows = []
            for g in range(group):
                sink = sink_ref[hk * group + g]
                sc = jnp.where(valid, s_all[g * bq:(g + 1) * bq, :], -jnp.inf)
                m = jnp.maximum(jnp.max(sc, axis=-1, keepdims=True), sink)
                e = jnp.exp(sc - m)
                denom = jnp.sum(e, axis=-1, keepdims=True) + jnp.exp(sink - m)
                p_rows.append((e / denom).astype(v_dup.dtype))
            p_stack = jnp.concatenate(p_rows, axis=0)
            o_all = jnp.dot(p_stack, v_dup, preferred_element_type=jnp.float32)
            for p in range(pairs_per_group):
                pair = hk * pairs_per_group + p
                o_pair = jnp.where(lo_q, o_all[(2 * p) * bq:(2 * p + 1) * bq, :],
                                   o_all[(2 * p + 1) * bq:(2 * p + 2) * bq, :])
                c0 = pair * pair_w
                o_ref[:, c0:c0 + pair_w] = (o_pair * sg_ref[:, c0:c0 + pair_w]).astype(o_ref.dtype)


def _attention(q, kv, sg_attn, sinks, *, n_kv, head_dim):
    s, d_attn = q.shape
    n_q = d_attn // head_dim
    group = n_q // n_kv
    assert head_dim * 2 == LANES and group % 2 == 0 and n_kv % 2 == 0
    bq = BLOCK_Q
    d_kv2 = kv.shape[1]
    block_bytes = 2 * bq * (d_attn * (2 + 4 + 2) + 2 * d_kv2 * 2)
    return pl.pallas_call(
        functools.partial(_attn_kernel, n_kv=n_kv, group=group, head_dim=head_dim),
        out_shape=jax.ShapeDtypeStruct((s, d_attn), jnp.bfloat16),
        grid=(s // bq,),
        in_specs=[pl.BlockSpec(memory_space=pltpu.SMEM),
                  pl.BlockSpec((bq, d_attn), lambda n: (n, 0)),
                  pl.BlockSpec((bq, d_kv2), lambda n: (n, 0)),
                  pl.BlockSpec((bq, d_kv2), lambda n: (jnp.maximum(n - 1, 0), 0)),
                  pl.BlockSpec((bq, d_attn), lambda n: (n, 0))],
        out_specs=pl.BlockSpec((bq, d_attn), lambda n: (n, 0)),
        compiler_params=pltpu.CompilerParams(
            dimension_semantics=("arbitrary",),
            vmem_limit_bytes=_vmem_limit(block_bytes + 16 * 1024 * 1024)),
        name="swa_attn",
    )(sinks, q, kv, kv, sg_attn)


def kernel(x, c, w_ada, b_ada, norm_gain, w_in, conv_w, conv_b, w_gate_a, b_gate_a, w_gate_x,
           b_gate_x, lru_lambda, attn_sinks, w_rnn_proj, w_attn_proj, w_out, final_norm_gain):
    bsz, seq, d = x.shape
    depth = w_in.shape[0]
    d_rnn = w_rnn_proj.shape[1]
    d_attn = w_attn_proj.shape[1]
    n_q = attn_sinks.shape[1]
    head_dim = d_attn // n_q
    d_kv = (w_in.shape[2] - 2 * d_rnn - 2 * d_attn - 2 * d) // 2
    n_kv = d_kv // head_dim
    assert bsz == 1

    xs = x.reshape(bsz * seq, d)
    bf = jnp.bfloat16
    for l in range(depth):
        mod = _modulation(c, w_ada[l], b_ada[l])
        shift, scale, gate = mod[:, :d], mod[:, d:2 * d], mod[:, 2 * d:]
        h = _norm_mod(xs, norm_gain[l].reshape(1, d), scale, shift)

        w = w_in[l].astype(bf)
        proj = functools.partial(_matmul, h, w)
        o = 0
        u_rnn = proj(col_offset=o, n_cols=d_rnn, out_dtype=jnp.float32, name="proj_u")
        o += d_rnn
        sg_rnn = proj(col_offset=o, n_cols=d_rnn, out_dtype=jnp.float32, epilogue=_ep_silu,
                      name="proj_g_rnn")
        o += d_rnn
        q = proj(col_offset=o, n_cols=d_attn, out_dtype=bf, name="proj_q")
        o += d_attn
        kv = proj(col_offset=o, n_cols=2 * d_kv, out_dtype=bf, name="proj_kv")
        o += 2 * d_kv
        sg_attn = proj(col_offset=o, n_cols=d_attn, out_dtype=jnp.float32, epilogue=_ep_silu,
                       name="proj_g_attn")
        o += d_attn
        sm_rnn = proj(col_offset=o, n_cols=d, out_dtype=jnp.float32, epilogue=_ep_sigmoid,
                      name="proj_m_rnn")
        o += d
        sm_attn = proj(col_offset=o, n_cols=d, out_dtype=jnp.float32, epilogue=_ep_sigmoid,
                       name="proj_m_attn")

        y_rnn = _rglru(u_rnn, sg_rnn, conv_w[l], conv_b[l], w_gate_a[l], b_gate_a[l],
                       w_gate_x[l], b_gate_x[l], lru_lambda[l])
        y_attn = _attention(q, kv, sg_attn, attn_sinks[l], n_kv=n_kv, head_dim=head_dim)

        r_part = _matmul(y_rnn, w_rnn_proj[l].astype(bf), col_offset=0, n_cols=d,
                         out_dtype=jnp.float32, epilogue=_ep_scale, tiles=(sm_rnn,),
                         name="rnn_proj")
        merged = _matmul(y_attn, w_attn_proj[l].astype(bf), col_offset=0, n_cols=d,
                         out_dtype=bf, epilogue=_ep_scale_add, tiles=(sm_attn, r_part),
                         name="attn_proj_merge")
        xs = _matmul(merged, w_out[l].astype(bf), col_offset=0, n_cols=d,
                     out_dtype=jnp.float32, epilogue=_ep_residual, tiles=(xs,), rows=(gate,),
                     name="out_proj")
    out = _final_norm(xs, final_norm_gain.reshape(1, d))
    return out.reshape(bsz, seq, d)
```

```python
import functools

import jax
import jax.numpy as jnp
from jax import lax
from jax.experimental import pallas as pl
from jax.experimental.pallas import tpu as pltpu

EPS = 1e-6
LRU_C = 8.0
WINDOW = 128
BLOCK_Q = 128

V7X_VMEM_BYTES = 64 * 1024 * 1024
VMEM_HEADROOM_BYTES = 6 * 1024 * 1024
LANES = 128
SUBLANES = 8


def _vmem_limit(block_bytes):
    need = int(block_bytes) + VMEM_HEADROOM_BYTES
    assert need <= V7X_VMEM_BYTES - 2 * 1024 * 1024, need
    return need


def _sigmoid(v):
    return 0.5 * jnp.tanh(0.5 * v) + 0.5


def _silu(v):
    return v * _sigmoid(v)


def _mod_kernel(c_ref, w_ref, b_ref, o_ref):
    c_act = _silu(c_ref[...])
    lhs = jnp.broadcast_to(c_act, (SUBLANES, c_act.shape[1]))
    acc = jnp.dot(lhs, w_ref[...], preferred_element_type=jnp.float32,
                  precision=lax.Precision.HIGHEST)
    o_ref[...] = acc[0:1, :] + b_ref[...]


def _modulation(c, w_ada, b_ada):
    d, n = w_ada.shape
    tn = 512
    return pl.pallas_call(
        _mod_kernel,
        out_shape=jax.ShapeDtypeStruct((1, n), jnp.float32),
        grid=(n // tn,),
        in_specs=[pl.BlockSpec((1, d), lambda j: (0, 0)),
                  pl.BlockSpec((d, tn), lambda j: (0, j)),
                  pl.BlockSpec((1, tn), lambda j: (0, j))],
        out_specs=pl.BlockSpec((1, tn), lambda j: (0, j)),
        compiler_params=pltpu.CompilerParams(
            dimension_semantics=("arbitrary",),
            vmem_limit_bytes=_vmem_limit(2 * d * tn * 4)),
        name="adaln_mod",
    )(c, w_ada, b_ada.reshape(1, n))


def _norm_mod_kernel(x_ref, gain_ref, scale_ref, shift_ref, o_ref):
    x = x_ref[...]
    ms = jnp.mean(x * x, axis=-1, keepdims=True)
    g = gain_ref[...] * (1.0 + scale_ref[...])
    o_ref[...] = (x * lax.rsqrt(ms + EPS) * g + shift_ref[...]).astype(o_ref.dtype)


def _norm_mod(x2, gain, scale, shift):
    s, d = x2.shape
    tm = 256
    row = pl.BlockSpec((1, d), lambda i: (0, 0))
    return pl.pallas_call(
        _norm_mod_kernel,
        out_shape=jax.ShapeDtypeStruct((s, d), jnp.bfloat16),
        grid=(s // tm,),
        in_specs=[pl.BlockSpec((tm, d), lambda i: (i, 0)), row, row, row],
        out_specs=pl.BlockSpec((tm, d), lambda i: (i, 0)),
        compiler_params=pltpu.CompilerParams(
            dimension_semantics=("arbitrary",),
            vmem_limit_bytes=_vmem_limit(2 * tm * d * (4 + 2) + 2 * tm * d * 4)),
        name="norm_mod",
    )(x2, gain, scale, shift)


def _ep_identity(acc):
    return acc


def _ep_silu(acc):
    return _silu(acc)


def _ep_sigmoid(acc):
    return _sigmoid(acc)


def _ep_residual(acc, x, gate):
    return x + gate * acc


def _matmul_kernel(*refs, epilogue, n_tiles, n_rows):
    lhs_ref, rhs_ref = refs[0], refs[1]
    tile_refs = refs[2:2 + n_tiles]
    row_refs = refs[2 + n_tiles:2 + n_tiles + n_rows]
    o_ref = refs[-1]
    acc = jnp.dot(lhs_ref[...], rhs_ref[...], preferred_element_type=jnp.float32)
    extras = [r[...] for r in tile_refs] + [r[...] for r in row_refs]
    o_ref[...] = epilogue(acc, *extras).astype(o_ref.dtype)


def _matmul(lhs, rhs, *, col_offset, n_cols, out_dtype, epilogue=_ep_identity,
            tiles=(), rows=(), tn=1024, name):
    m, k = lhs.shape
    tm = 512 if tiles else 1024
    assert rhs.shape[0] == k and col_offset % tn == 0 and n_cols % tn == 0 and m % tm == 0
    off = col_offset // tn
    in_specs = [pl.BlockSpec((tm, k), lambda i, j: (i, 0)),
                pl.BlockSpec((k, tn), lambda i, j: (0, off + j))]
    in_specs += [pl.BlockSpec((tm, tn), lambda i, j: (i, j)) for _ in tiles]
    in_specs += [pl.BlockSpec((1, tn), lambda i, j: (0, j)) for _ in rows]
    out_bytes = jnp.dtype(out_dtype).itemsize
    tile_bytes = sum(t.dtype.itemsize for t in tiles)
    block_bytes = 2 * (tm * k * lhs.dtype.itemsize + k * tn * rhs.dtype.itemsize
                       + tm * tn * (out_bytes + tile_bytes)) + 2 * tm * tn * 4
    return pl.pallas_call(
        functools.partial(_matmul_kernel, epilogue=epilogue, n_tiles=len(tiles),
                          n_rows=len(rows)),
        out_shape=jax.ShapeDtypeStruct((m, n_cols), out_dtype),
        grid=(m // tm, n_cols // tn),
        in_specs=in_specs,
        out_specs=pl.BlockSpec((tm, tn), lambda i, j: (i, j)),
        compiler_params=pltpu.CompilerParams(
            dimension_semantics=("arbitrary", "arbitrary"),
            vmem_limit_bytes=_vmem_limit(block_bytes)),
        name=name,
    )(lhs, rhs, *tiles, *rows)


def _merge_kernel(yr_ref, ya_ref, wr_ref, wa_ref, smr_ref, sma_ref, o_ref):
    p_rnn = jnp.dot(yr_ref[...], wr_ref[...], preferred_element_type=jnp.float32)
    p_attn = jnp.dot(ya_ref[...], wa_ref[...], preferred_element_type=jnp.float32)
    o_ref[...] = (smr_ref[...] * p_rnn + sma_ref[...] * p_attn).astype(o_ref.dtype)


def _merge(y_rnn, y_attn, w_rnn, w_attn, sm_rnn, sm_attn):
    m, k = y_rnn.shape
    n = w_rnn.shape[1]
    tm, tn = 512, 512
    lhs = pl.BlockSpec((tm, k), lambda i, j: (i, 0))
    rhs = pl.BlockSpec((k, tn), lambda i, j: (0, j))
    tile = pl.BlockSpec((tm, tn), lambda i, j: (i, j))
    block_bytes = 2 * (2 * tm * k * 2 + 2 * k * tn * 2 + tm * tn * (4 + 4 + 2)) + 3 * tm * tn * 4
    return pl.pallas_call(
        _merge_kernel,
        out_shape=jax.ShapeDtypeStruct((m, n), jnp.bfloat16),
        grid=(m // tm, n // tn),
        in_specs=[lhs, lhs, rhs, rhs, tile, tile],
        out_specs=tile,
        compiler_params=pltpu.CompilerParams(
            dimension_semantics=("arbitrary", "arbitrary"),
            vmem_limit_bytes=_vmem_limit(block_bytes)),
        name="merge_proj",
    )(y_rnn, y_attn, w_rnn, w_attn, sm_rnn, sm_attn)


def _out_norm_kernel(m_ref, w_ref, x_ref, gate_ref, gain_ref, o_ref, res_ref):
    j = pl.program_id(1)
    n_j = pl.num_programs(1)
    tn = w_ref.shape[1]
    acc = jnp.dot(m_ref[...], w_ref[...], preferred_element_type=jnp.float32)
    res_ref[j] = x_ref[...] + gate_ref[...] * acc

    @pl.when(j == n_j - 1)
    def _():
        n_tiles = res_ref.shape[0]
        ss = jnp.sum(res_ref[0] * res_ref[0], axis=-1, keepdims=True)
        for t in range(1, n_tiles):
            ss = ss + jnp.sum(res_ref[t] * res_ref[t], axis=-1, keepdims=True)
        inv = lax.rsqrt(ss / (n_tiles * tn) + EPS)
        for t in range(n_tiles):
            o_ref[:, t * tn:(t + 1) * tn] = res_ref[t] * inv * gain_ref[:, t * tn:(t + 1) * tn]


def _out_norm(merged, w, x2, gate, gain):
    m, k = merged.shape
    n = w.shape[1]
    tm, tn = 512, 512
    block_bytes = (2 * (tm * k * 2 + k * tn * 2 + tm * tn * 4 + tm * n * 4) + tm * n * 4
                   + 3 * tm * tn * 4)
    return pl.pallas_call(
        _out_norm_kernel,
        out_shape=jax.ShapeDtypeStruct((m, n), jnp.float32),
        grid=(m // tm, n // tn),
        in_specs=[pl.BlockSpec((tm, k), lambda i, j: (i, 0)),
                  pl.BlockSpec((k, tn), lambda i, j: (0, j)),
                  pl.BlockSpec((tm, tn), lambda i, j: (i, j)),
                  pl.BlockSpec((1, tn), lambda i, j: (0, j)),
                  pl.BlockSpec((1, n), lambda i, j: (0, 0))],
        out_specs=pl.BlockSpec((tm, n), lambda i, j: (i, 0)),
        scratch_shapes=[pltpu.VMEM((n // tn, tm, tn), jnp.float32)],
        compiler_params=pltpu.CompilerParams(
            dimension_semantics=("arbitrary", "arbitrary"),
            vmem_limit_bytes=_vmem_limit(block_bytes)),
        name="out_norm",
    )(merged, w, x2, gate, gain)


RNN_SUB_ROWS = 256
RNN_BLOCK_ROWS = 2048
RNN_HALO_GROUPS = 4
SQRT_FLOOR = 1e-30


def _slabs(ref, rows, n_slabs):
    return jnp.concatenate([ref[l, rows, :] for l in range(n_slabs)], axis=1)


def _rglru_kernel(u_ref, sg_ref, cw_ref, cb_ref, wa_ref, ba_ref, wx_ref, bx_ref, lam_ref,
                  y_ref, uperm, hbuf, pbuf, carry_ref, *, conv_width):
    ts = RNN_SUB_ROWS
    tc = ts // SUBLANES
    width = u_ref.shape[1]
    n_slabs = width // LANES
    data0 = RNN_HALO_GROUPS * SUBLANES
    taps = conv_width - 1
    assert taps < RNN_HALO_GROUPS and taps <= tc

    @pl.when(pl.program_id(1) == 0)
    def _():
        uperm[...] = jnp.zeros_like(uperm)
        carry_ref[...] = jnp.zeros_like(carry_ref)

    neg = -lam_ref[...]
    softplus_neg = jnp.maximum(neg, 0.0) + jnp.log1p(jnp.exp(-jnp.abs(neg)))
    half_decay = (-0.5 * LRU_C) * softplus_neg
    half_ba = 0.5 * ba_ref[...]
    half_bx = 0.5 * bx_ref[...]
    cw = cw_ref[...]
    sub = lax.broadcasted_iota(jnp.int32, (SUBLANES, width), 0)

    def group(j):
        return pl.ds(data0 + j * SUBLANES, SUBLANES)

    def sub_tile(s, _):
        r0 = pl.multiple_of(s * ts, ts)
        prev = [_slabs(uperm, group(tc - k), n_slabs) for k in range(1, taps + 1)]
        for c in range(SUBLANES):
            for l in range(n_slabs):
                uperm[l, pl.ds(data0 + c, tc, stride=SUBLANES), :] = (
                    u_ref[pl.ds(r0 + c * tc, tc), l * LANES:(l + 1) * LANES])
        for k in range(1, taps + 1):
            cur = _slabs(uperm, group(tc - k), n_slabs)
            halo = jnp.where(sub == 0, pltpu.roll(prev[k - 1], 1, axis=0),
                             pltpu.roll(cur, 1, axis=0))
            for l in range(n_slabs):
                uperm[l, group(-k), :] = halo[:, l * LANES:(l + 1) * LANES]
        u = cb_ref[...]
        for k in range(conv_width):
            first = data0 - (taps - k) * SUBLANES
            u = u + cw[k:k + 1, :] * _slabs(uperm, pl.ds(first, ts), n_slabs)
        ub = u.astype(jnp.bfloat16)
        t_r = jnp.tanh(jnp.dot(ub, wa_ref[...], preferred_element_type=jnp.float32) + half_ba)
        t_i = jnp.tanh(jnp.dot(ub, wx_ref[...], preferred_element_type=jnp.float32) + half_bx)
        log_a = half_decay * t_r + half_decay
        a = jnp.exp(log_a)
        half_u = 0.5 * u
        gated_u = half_u * t_i + half_u
        one_m_a2 = 1.0 - a * a
        b = (one_m_a2 * lax.rsqrt(jnp.maximum(one_m_a2, SQRT_FLOOR))) * gated_u

        h_loc = b[0:SUBLANES]
        p_cum = a[0:SUBLANES]
        for j in range(tc):
            if j > 0:
                aj = a[j * SUBLANES:(j + 1) * SUBLANES]
                h_loc = aj * h_loc + b[j * SUBLANES:(j + 1) * SUBLANES]
                p_cum = aj * p_cum
            pbuf[j * SUBLANES:(j + 1) * SUBLANES, :] = p_cum
            for l in range(n_slabs):
                hbuf[l, j * SUBLANES:(j + 1) * SUBLANES, :] = h_loc[:, l * LANES:(l + 1) * LANES]

        a_inc, b_inc = p_cum, h_loc
        shift = 1
        while shift < SUBLANES:
            keep = sub >= shift
            a_prev = jnp.where(keep, pltpu.roll(a_inc, shift, axis=0), 1.0)
            b_prev = jnp.where(keep, pltpu.roll(b_inc, shift, axis=0), 0.0)
            b_inc = a_inc * b_prev + b_inc
            a_inc = a_inc * a_prev
            shift *= 2
        h_in = jnp.broadcast_to(carry_ref[...], (SUBLANES, width))
        chunk_end = b_inc + a_inc * h_in
        entry = jnp.where(sub == 0, h_in, pltpu.roll(chunk_end, 1, axis=0))
        carry_ref[...] = chunk_end[SUBLANES - 1:SUBLANES, :]

        for j in range(tc):
            rows = slice(j * SUBLANES, (j + 1) * SUBLANES)
            h = _slabs(hbuf, rows, n_slabs) + pbuf[rows, :] * entry
            for l in range(n_slabs):
                hbuf[l, rows, :] = h[:, l * LANES:(l + 1) * LANES]
        for c in range(SUBLANES):
            rows = pl.ds(r0 + c * tc, tc)
            h_nat = _slabs(hbuf, pl.ds(c, tc, stride=SUBLANES), n_slabs)
            y_ref[rows, :] = (h_nat * sg_ref[rows, :]).astype(y_ref.dtype)
        return 0

    lax.fori_loop(0, u_ref.shape[0] // ts, sub_tile, 0, unroll=2)


def _rglru(u_rnn, sg_rnn, conv_w, conv_b, w_gate_a, b_gate_a, w_gate_x, b_gate_x, lam):
    s, d = u_rnn.shape
    nb, bw, _ = w_gate_a.shape
    conv_width = conv_w.shape[0]
    tb = RNN_BLOCK_ROWS
    seq = pl.BlockSpec((tb, bw), lambda c, t: (t, c))
    vec = pl.BlockSpec((1, bw), lambda c, t: (0, c))
    gate_w = pl.BlockSpec((None, bw, bw), lambda c, t: (c, 0, 0))
    gate_b = pl.BlockSpec((None, 1, bw), lambda c, t: (c, 0, 0))
    block_bytes = (2 * tb * bw * (4 + 4 + 2) + 4 * bw * bw * 2
                   + 3 * (RNN_SUB_ROWS + RNN_HALO_GROUPS * SUBLANES) * bw * 4)
    return pl.pallas_call(
        functools.partial(_rglru_kernel, conv_width=conv_width),
        out_shape=jax.ShapeDtypeStruct((s, d), jnp.bfloat16),
        grid=(nb, s // tb),
        in_specs=[seq, seq,
                  pl.BlockSpec((conv_width, bw), lambda c, t: (0, c)), vec,
                  gate_w, gate_b, gate_w, gate_b, vec],
        out_specs=seq,
        scratch_shapes=[
            pltpu.VMEM((bw // LANES, RNN_SUB_ROWS + RNN_HALO_GROUPS * SUBLANES, LANES),
                       jnp.float32),
            pltpu.VMEM((bw // LANES, RNN_SUB_ROWS, LANES), jnp.float32),
            pltpu.VMEM((RNN_SUB_ROWS, bw), jnp.float32),
            pltpu.VMEM((1, bw), jnp.float32)],
        compiler_params=pltpu.CompilerParams(
            dimension_semantics=("arbitrary", "arbitrary"),
            vmem_limit_bytes=_vmem_limit(block_bytes + 16 * 1024 * 1024)),
        name="rglru",
    )(u_rnn, sg_rnn, conv_w, conv_b.reshape(1, d),
      (0.5 * w_gate_a).astype(jnp.bfloat16), b_gate_a.reshape(nb, 1, bw),
      (0.5 * w_gate_x).astype(jnp.bfloat16), b_gate_x.reshape(nb, 1, bw), lam.reshape(1, d))


def _swap_halves(v):
    half = v.shape[1] // 2
    return jnp.concatenate([v[:, half:], v[:, :half]], axis=1)


def _attn_kernel(sink_ref, q_ref, kv_ref, kvp_ref, sg_ref, o_ref, *, n_kv, group, head_dim):
    bq = q_ref.shape[0]
    pair_w = 2 * head_dim
    pairs_per_group = group // 2
    d_kv = n_kv * head_dim
    n = pl.program_id(0)

    q_pos = lax.broadcasted_iota(jnp.int32, (bq, 2 * bq), 0)
    k_pos = lax.broadcasted_iota(jnp.int32, (bq, 2 * bq), 1)
    dist = q_pos + bq - k_pos
    valid = (dist >= 0) & (dist < WINDOW) & ((k_pos >= bq) | (n > 0))
    lane = lax.broadcasted_iota(jnp.int32, (bq, pair_w), 1)
    lo_q = lane < head_dim
    lane_kv = lax.broadcasted_iota(jnp.int32, (2 * bq, pair_w), 1)
    lo_kv = lane_kv < head_dim
    scale = head_dim ** -0.5
    zero_q = jnp.zeros((bq, pair_w), q_ref.dtype)

    for kv_pair in range(n_kv // 2):
        cols = slice(kv_pair * pair_w, (kv_pair + 1) * pair_w)
        k_two = jnp.concatenate([kvp_ref[:, cols], kv_ref[:, cols]], axis=0)
        v_cols = slice(d_kv + kv_pair * pair_w, d_kv + (kv_pair + 1) * pair_w)
        v_two = jnp.concatenate([kvp_ref[:, v_cols], kv_ref[:, v_cols]], axis=0)
        k_swapped = _swap_halves(k_two)
        v_swapped = _swap_halves(v_two)
        for parity in range(2):
            hk = 2 * kv_pair + parity
            if parity == 0:
                k_dup = jnp.where(lo_kv, k_two, k_swapped)
                v_dup = jnp.where(lo_kv, v_two, v_swapped)
            else:
                k_dup = jnp.where(lo_kv, k_swapped, k_two)
                v_dup = jnp.where(lo_kv, v_swapped, v_two)
            k_dup = k_dup * jnp.asarray(scale, k_dup.dtype)
            q_rows = []
            for p in range(pairs_per_group):
                pair = hk * pairs_per_group + p
                qp = q_ref[:, pair * pair_w:(pair + 1) * pair_w]
                q_rows.append(jnp.where(lo_q, qp, zero_q))
                q_rows.append(jnp.where(lo_q, zero_q, qp))
            q_stack = jnp.concatenate(q_rows, axis=0)
            s_all = lax.dot_general(q_stack, k_dup, (((1,), (1,)), ((), ())),
                                    preferred_element_type=jnp.float32)
            p_rows, inv_rows = [], []
            for g in range(group):
                sink = sink_ref[hk * group + g]
                sc = jnp.where(valid, s_all[g * bq:(g + 1) * bq, :], -jnp.inf)
                m = jnp.maximum(jnp.max(sc, axis=-1, keepdims=True), sink)
                e = jnp.exp(sc - m)
                denom = jnp.sum(e, axis=-1, keepdims=True) + jnp.exp(sink - m)
                p_rows.append(e.astype(v_dup.dtype))
                inv_rows.append(1.0 / denom)
            p_stack = jnp.concatenate(p_rows, axis=0)
            o_all = jnp.dot(p_stack, v_dup, preferred_element_type=jnp.float32)
            for p in range(pairs_per_group):
                pair = hk * pairs_per_group + p
                o_even = o_all[(2 * p) * bq:(2 * p + 1) * bq, :] * inv_rows[2 * p]
                o_odd = o_all[(2 * p + 1) * bq:(2 * p + 2) * bq, :] * inv_rows[2 * p + 1]
                o_pair = jnp.where(lo_q, o_even, o_odd)
                c0 = pair * pair_w
                o_ref[:, c0:c0 + pair_w] = (o_pair * sg_ref[:, c0:c0 + pair_w]).astype(o_ref.dtype)


def _attention(q, kv, sg_attn, sinks, *, n_kv, head_dim):
    s, d_attn = q.shape
    n_q = d_attn // head_dim
    group = n_q // n_kv
    assert head_dim * 2 == LANES and group % 2 == 0 and n_kv % 2 == 0
    bq = BLOCK_Q
    d_kv2 = kv.shape[1]
    block_bytes = 2 * bq * (d_attn * (2 + 4 + 2) + 2 * d_kv2 * 2)
    return pl.pallas_call(
        functools.partial(_attn_kernel, n_kv=n_kv, group=group, head_dim=head_dim),
        out_shape=jax.ShapeDtypeStruct((s, d_attn), jnp.bfloat16),
        grid=(s // bq,),
        in_specs=[pl.BlockSpec(memory_space=pltpu.SMEM),
                  pl.BlockSpec((bq, d_attn), lambda n: (n, 0)),
                  pl.BlockSpec((bq, d_kv2), lambda n: (n, 0)),
                  pl.BlockSpec((bq, d_kv2), lambda n: (jnp.maximum(n - 1, 0), 0)),
                  pl.BlockSpec((bq, d_attn), lambda n: (n, 0))],
        out_specs=pl.BlockSpec((bq, d_attn), lambda n: (n, 0)),
        compiler_params=pltpu.CompilerParams(
            dimension_semantics=("arbitrary",),
            vmem_limit_bytes=_vmem_limit(block_bytes + 16 * 1024 * 1024)),
        name="swa_attn",
    )(sinks, q, kv, kv, sg_attn)


def kernel(x, c, w_ada, b_ada, norm_gain, w_in, conv_w, conv_b, w_gate_a, b_gate_a, w_gate_x,
           b_gate_x, lru_lambda, attn_sinks, w_rnn_proj, w_attn_proj, w_out, final_norm_gain):
    bsz, seq, d = x.shape
    depth = w_in.shape[0]
    d_rnn = w_rnn_proj.shape[1]
    d_attn = w_attn_proj.shape[1]
    n_q = attn_sinks.shape[1]
    head_dim = d_attn // n_q
    d_kv = (w_in.shape[2] - 2 * d_rnn - 2 * d_attn - 2 * d) // 2
    n_kv = d_kv // head_dim
    assert bsz == 1

    xs = x.reshape(bsz * seq, d)
    bf = jnp.bfloat16
    for l in range(depth):
        mod = _modulation(c, w_ada[l], b_ada[l])
        shift, scale, gate = mod[:, :d], mod[:, d:2 * d], mod[:, 2 * d:]
        h = _norm_mod(xs, norm_gain[l].reshape(1, d), scale, shift)

        w = w_in[l].astype(bf)
        proj = functools.partial(_matmul, h, w)
        o = 0
        u_rnn = proj(col_offset=o, n_cols=d_rnn, out_dtype=jnp.float32, name="proj_u")
        o += d_rnn
        sg_rnn = proj(col_offset=o, n_cols=d_rnn, out_dtype=jnp.float32, epilogue=_ep_silu,
                      name="proj_g_rnn")
        o += d_rnn
        q = proj(col_offset=o, n_cols=d_attn, out_dtype=bf, name="proj_q")
        o += d_attn
        kv = proj(col_offset=o, n_cols=2 * d_kv, out_dtype=bf, name="proj_kv")
        o += 2 * d_kv
        sg_attn = proj(col_offset=o, n_cols=d_attn, out_dtype=jnp.float32, epilogue=_ep_silu,
                       name="proj_g_attn")
        o += d_attn
        sm_rnn = proj(col_offset=o, n_cols=d, out_dtype=jnp.float32, epilogue=_ep_sigmoid,
                      name="proj_m_rnn")
        o += d
        sm_attn = proj(col_offset=o, n_cols=d, out_dtype=jnp.float32, epilogue=_ep_sigmoid,
                       name="proj_m_attn")

        y_rnn = _rglru(u_rnn, sg_rnn, conv_w[l], conv_b[l], w_gate_a[l], b_gate_a[l],
                       w_gate_x[l], b_gate_x[l], lru_lambda[l])
        y_attn = _attention(q, kv, sg_attn, attn_sinks[l], n_kv=n_kv, head_dim=head_dim)

        merged = _merge(y_rnn, y_attn, w_rnn_proj[l].astype(bf), w_attn_proj[l].astype(bf),
                        sm_rnn, sm_attn)
        if l + 1 < depth:
            xs = _matmul(merged, w_out[l].astype(bf), col_offset=0, n_cols=d,
                         out_dtype=jnp.float32, epilogue=_ep_residual, tiles=(xs,),
                         rows=(gate,), name="out_proj")
        else:
            xs = _out_norm(merged, w_out[l].astype(bf), xs, gate, final_norm_gain.reshape(1, d))
    return xs.reshape(bsz, seq, d)
```

```python
import functools

import jax
import jax.numpy as jnp
from jax import lax
from jax.experimental import pallas as pl
from jax.experimental.pallas import tpu as pltpu

EPS = 1e-6
LRU_C = 8.0
WINDOW = 128
BLOCK_Q = 128
ATTN_BLOCKS_PER_STEP = 2

V7X_VMEM_BYTES = 64 * 1024 * 1024
VMEM_HEADROOM_BYTES = 6 * 1024 * 1024
LANES = 128
SUBLANES = 8


def _vmem_limit(block_bytes):
    need = int(block_bytes) + VMEM_HEADROOM_BYTES
    assert need <= V7X_VMEM_BYTES - 2 * 1024 * 1024, need
    return need


def _sigmoid(v):
    return 0.5 * jnp.tanh(0.5 * v) + 0.5


def _silu(v):
    return v * _sigmoid(v)


def _mod_kernel(c_ref, w_ref, b_ref, o_ref):
    c_act = _silu(c_ref[...])
    lhs = jnp.broadcast_to(c_act, (SUBLANES, c_act.shape[1]))
    acc = jnp.dot(lhs, w_ref[...], preferred_element_type=jnp.float32,
                  precision=lax.Precision.HIGHEST)
    o_ref[...] = acc[0:1, :] + b_ref[...]


def _modulation(c, w_ada, b_ada):
    d, n = w_ada.shape
    tn = 512
    return pl.pallas_call(
        _mod_kernel,
        out_shape=jax.ShapeDtypeStruct((1, n), jnp.float32),
        grid=(n // tn,),
        in_specs=[pl.BlockSpec((1, d), lambda j: (0, 0)),
                  pl.BlockSpec((d, tn), lambda j: (0, j)),
                  pl.BlockSpec((1, tn), lambda j: (0, j))],
        out_specs=pl.BlockSpec((1, tn), lambda j: (0, j)),
        compiler_params=pltpu.CompilerParams(
            dimension_semantics=("arbitrary",),
            vmem_limit_bytes=_vmem_limit(2 * d * tn * 4)),
        name="adaln_mod",
    )(c, w_ada, b_ada.reshape(1, n))


def _norm_mod_kernel(x_ref, gain_ref, scale_ref, shift_ref, o_ref):
    x = x_ref[...]
    ms = jnp.mean(x * x, axis=-1, keepdims=True)
    g = gain_ref[...] * (1.0 + scale_ref[...])
    o_ref[...] = (x * lax.rsqrt(ms + EPS) * g + shift_ref[...]).astype(o_ref.dtype)


def _norm_mod(x2, gain, scale, shift):
    s, d = x2.shape
    tm = 256
    row = pl.BlockSpec((1, d), lambda i: (0, 0))
    return pl.pallas_call(
        _norm_mod_kernel,
        out_shape=jax.ShapeDtypeStruct((s, d), jnp.bfloat16),
        grid=(s // tm,),
        in_specs=[pl.BlockSpec((tm, d), lambda i: (i, 0)), row, row, row],
        out_specs=pl.BlockSpec((tm, d), lambda i: (i, 0)),
        compiler_params=pltpu.CompilerParams(
            dimension_semantics=("arbitrary",),
            vmem_limit_bytes=_vmem_limit(2 * tm * d * (4 + 2) + 2 * tm * d * 4)),
        name="norm_mod",
    )(x2, gain, scale, shift)


def _ep_identity(acc):
    return acc


def _ep_silu(acc):
    return _silu(acc)


def _ep_sigmoid(acc):
    return _sigmoid(acc)


def _ep_residual(acc, x, gate):
    return x + gate * acc


def _matmul_kernel(*refs, epilogue, n_tiles, n_rows):
    lhs_ref, rhs_ref = refs[0], refs[1]
    tile_refs = refs[2:2 + n_tiles]
    row_refs = refs[2 + n_tiles:2 + n_tiles + n_rows]
    o_ref = refs[-1]
    acc = jnp.dot(lhs_ref[...], rhs_ref[...], preferred_element_type=jnp.float32)
    extras = [r[...] for r in tile_refs] + [r[...] for r in row_refs]
    o_ref[...] = epilogue(acc, *extras).astype(o_ref.dtype)


def _matmul(lhs, rhs, *, col_offset, n_cols, out_dtype, epilogue=_ep_identity,
            tiles=(), rows=(), tn=1024, name):
    m, k = lhs.shape
    tm = 512 if tiles else 1024
    assert rhs.shape[0] == k and col_offset % tn == 0 and n_cols % tn == 0 and m % tm == 0
    off = col_offset // tn
    in_specs = [pl.BlockSpec((tm, k), lambda i, j: (i, 0)),
                pl.BlockSpec((k, tn), lambda i, j: (0, off + j))]
    in_specs += [pl.BlockSpec((tm, tn), lambda i, j: (i, j)) for _ in tiles]
    in_specs += [pl.BlockSpec((1, tn), lambda i, j: (0, j)) for _ in rows]
    out_bytes = jnp.dtype(out_dtype).itemsize
    tile_bytes = sum(t.dtype.itemsize for t in tiles)
    block_bytes = 2 * (tm * k * lhs.dtype.itemsize + k * tn * rhs.dtype.itemsize
                       + tm * tn * (out_bytes + tile_bytes)) + 2 * tm * tn * 4
    return pl.pallas_call(
        functools.partial(_matmul_kernel, epilogue=epilogue, n_tiles=len(tiles),
                          n_rows=len(rows)),
        out_shape=jax.ShapeDtypeStruct((m, n_cols), out_dtype),
        grid=(m // tm, n_cols // tn),
        in_specs=in_specs,
        out_specs=pl.BlockSpec((tm, tn), lambda i, j: (i, j)),
        compiler_params=pltpu.CompilerParams(
            dimension_semantics=("arbitrary", "arbitrary"),
            vmem_limit_bytes=_vmem_limit(block_bytes)),
        name=name,
    )(lhs, rhs, *tiles, *rows)


def _merge_kernel(yr_ref, ya_ref, wr_ref, wa_ref, smr_ref, sma_ref, o_ref):
    p_rnn = jnp.dot(yr_ref[...], wr_ref[...], preferred_element_type=jnp.float32)
    p_attn = jnp.dot(ya_ref[...], wa_ref[...], preferred_element_type=jnp.float32)
    o_ref[...] = (smr_ref[...] * p_rnn + sma_ref[...] * p_attn).astype(o_ref.dtype)


def _merge(y_rnn, y_attn, w_rnn, w_attn, sm_rnn, sm_attn):
    m, k = y_rnn.shape
    n = w_rnn.shape[1]
    tm, tn = 1024, 256
    lhs = pl.BlockSpec((tm, k), lambda i, j: (i, 0))
    rhs = pl.BlockSpec((k, tn), lambda i, j: (0, j))
    tile = pl.BlockSpec((tm, tn), lambda i, j: (i, j))
    block_bytes = 2 * (2 * tm * k * 2 + 2 * k * tn * 2 + tm * tn * (4 + 4 + 2)) + 3 * tm * tn * 4
    return pl.pallas_call(
        _merge_kernel,
        out_shape=jax.ShapeDtypeStruct((m, n), jnp.bfloat16),
        grid=(m // tm, n // tn),
        in_specs=[lhs, lhs, rhs, rhs, tile, tile],
        out_specs=tile,
        compiler_params=pltpu.CompilerParams(
            dimension_semantics=("arbitrary", "arbitrary"),
            vmem_limit_bytes=_vmem_limit(block_bytes)),
        name="merge_proj",
    )(y_rnn, y_attn, w_rnn, w_attn, sm_rnn, sm_attn)


def _out_norm_kernel(m_ref, w_ref, x_ref, gate_ref, gain_ref, o_ref, res_ref):
    j = pl.program_id(1)
    n_j = pl.num_programs(1)
    tn = w_ref.shape[1]
    acc = jnp.dot(m_ref[...], w_ref[...], preferred_element_type=jnp.float32)
    res_ref[j] = x_ref[...] + gate_ref[...] * acc

    @pl.when(j == n_j - 1)
    def _():
        n_tiles = res_ref.shape[0]
        ss = jnp.sum(res_ref[0] * res_ref[0], axis=-1, keepdims=True)
        for t in range(1, n_tiles):
            ss = ss + jnp.sum(res_ref[t] * res_ref[t], axis=-1, keepdims=True)
        inv = lax.rsqrt(ss / (n_tiles * tn) + EPS)
        for t in range(n_tiles):
            o_ref[:, t * tn:(t + 1) * tn] = res_ref[t] * inv * gain_ref[:, t * tn:(t + 1) * tn]


def _out_norm(merged, w, x2, gate, gain):
    m, k = merged.shape
    n = w.shape[1]
    tm, tn = 512, 512
    block_bytes = (2 * (tm * k * 2 + k * tn * 2 + tm * tn * 4 + tm * n * 4) + tm * n * 4
                   + 3 * tm * tn * 4)
    return pl.pallas_call(
        _out_norm_kernel,
        out_shape=jax.ShapeDtypeStruct((m, n), jnp.float32),
        grid=(m // tm, n // tn),
        in_specs=[pl.BlockSpec((tm, k), lambda i, j: (i, 0)),
                  pl.BlockSpec((k, tn), lambda i, j: (0, j)),
                  pl.BlockSpec((tm, tn), lambda i, j: (i, j)),
                  pl.BlockSpec((1, tn), lambda i, j: (0, j)),
                  pl.BlockSpec((1, n), lambda i, j: (0, 0))],
        out_specs=pl.BlockSpec((tm, n), lambda i, j: (i, 0)),
        scratch_shapes=[pltpu.VMEM((n // tn, tm, tn), jnp.float32)],
        compiler_params=pltpu.CompilerParams(
            dimension_semantics=("arbitrary", "arbitrary"),
            vmem_limit_bytes=_vmem_limit(block_bytes)),
        name="out_norm",
    )(merged, w, x2, gate, gain)


RNN_SUB_ROWS = 256
RNN_BLOCK_ROWS = 1024
RNN_HALO_GROUPS = 4
SQRT_FLOOR = 1e-30


def _slabs(ref, rows, n_slabs):
    return jnp.concatenate([ref[l, rows, :] for l in range(n_slabs)], axis=1)


def _rnn_branch_kernel(h_ref, wu_ref, wg_ref, cw_ref, cb_ref, wa_ref, ba_ref, wx_ref, bx_ref,
                       lam_ref, y_ref, uperm, hbuf, pbuf, carry_ref, *, conv_width):
    ts = RNN_SUB_ROWS
    tc = ts // SUBLANES
    width = wu_ref.shape[1]
    n_slabs = width // LANES
    data0 = RNN_HALO_GROUPS * SUBLANES
    taps = conv_width - 1
    assert taps < RNN_HALO_GROUPS and taps <= tc

    @pl.when(pl.program_id(1) == 0)
    def _():
        uperm[...] = jnp.zeros_like(uperm)
        carry_ref[...] = jnp.zeros_like(carry_ref)

    neg = -lam_ref[...]
    softplus_neg = jnp.maximum(neg, 0.0) + jnp.log1p(jnp.exp(-jnp.abs(neg)))
    half_decay = (-0.5 * LRU_C) * softplus_neg
    half_ba = 0.5 * ba_ref[...]
    half_bx = 0.5 * bx_ref[...]
    cw = cw_ref[...]
    sub = lax.broadcasted_iota(jnp.int32, (SUBLANES, width), 0)

    def group(j):
        return pl.ds(data0 + j * SUBLANES, SUBLANES)

    for s in range(h_ref.shape[0] // ts):
        r0 = s * ts
        h_rows = h_ref[r0:r0 + ts, :]
        u_nat = jnp.dot(h_rows, wu_ref[...], preferred_element_type=jnp.float32)
        sg_nat = _silu(jnp.dot(h_rows, wg_ref[...], preferred_element_type=jnp.float32))
        prev = [_slabs(uperm, group(tc - k), n_slabs) for k in range(1, taps + 1)]
        for c in range(SUBLANES):
            for l in range(n_slabs):
                uperm[l, pl.ds(data0 + c, tc, stride=SUBLANES), :] = (
                    u_nat[c * tc:(c + 1) * tc, l * LANES:(l + 1) * LANES])
        for k in range(1, taps + 1):
            cur = _slabs(uperm, group(tc - k), n_slabs)
            halo = jnp.where(sub == 0, pltpu.roll(prev[k - 1], 1, axis=0),
                             pltpu.roll(cur, 1, axis=0))
            for l in range(n_slabs):
                uperm[l, group(-k), :] = halo[:, l * LANES:(l + 1) * LANES]
        u = cb_ref[...]
        for k in range(conv_width):
            first = data0 - (taps - k) * SUBLANES
            u = u + cw[k:k + 1, :] * _slabs(uperm, pl.ds(first, ts), n_slabs)
        ub = u.astype(jnp.bfloat16)
        t_r = jnp.tanh(jnp.dot(ub, wa_ref[...], preferred_element_type=jnp.float32) + half_ba)
        t_i = jnp.tanh(jnp.dot(ub, wx_ref[...], preferred_element_type=jnp.float32) + half_bx)
        log_a = half_decay * t_r + half_decay
        a = jnp.exp(log_a)
        half_u = 0.5 * u
        gated_u = half_u * t_i + half_u
        one_m_a2 = 1.0 - a * a
        b = (one_m_a2 * lax.rsqrt(jnp.maximum(one_m_a2, SQRT_FLOOR))) * gated_u

        h_loc = b[0:SUBLANES]
        p_cum = a[0:SUBLANES]
        for j in range(tc):
            if j > 0:
                aj = a[j * SUBLANES:(j + 1) * SUBLANES]
                h_loc = aj * h_loc + b[j * SUBLANES:(j + 1) * SUBLANES]
                p_cum = aj * p_cum
            pbuf[j * SUBLANES:(j + 1) * SUBLANES, :] = p_cum
            for l in range(n_slabs):
                hbuf[l, j * SUBLANES:(j + 1) * SUBLANES, :] = h_loc[:, l * LANES:(l + 1) * LANES]

        a_inc, b_inc = p_cum, h_loc
        shift = 1
        while shift < SUBLANES:
            keep = sub >= shift
            a_prev = jnp.where(keep, pltpu.roll(a_inc, shift, axis=0), 1.0)
            b_prev = jnp.where(keep, pltpu.roll(b_inc, shift, axis=0), 0.0)
            b_inc = a_inc * b_prev + b_inc
            a_inc = a_inc * a_prev
            shift *= 2
        h_in = jnp.broadcast_to(carry_ref[...], (SUBLANES, width))
        chunk_end = b_inc + a_inc * h_in
        entry = jnp.where(sub == 0, h_in, pltpu.roll(chunk_end, 1, axis=0))
        carry_ref[...] = chunk_end[SUBLANES - 1:SUBLANES, :]

        for j in range(tc):
            rows = slice(j * SUBLANES, (j + 1) * SUBLANES)
            h = _slabs(hbuf, rows, n_slabs) + pbuf[rows, :] * entry
            for l in range(n_slabs):
                hbuf[l, rows, :] = h[:, l * LANES:(l + 1) * LANES]
        for c in range(SUBLANES):
            h_nat = _slabs(hbuf, pl.ds(c, tc, stride=SUBLANES), n_slabs)
            y_ref[r0 + c * tc:r0 + (c + 1) * tc, :] = (
                h_nat * sg_nat[c * tc:(c + 1) * tc, :]).astype(y_ref.dtype)


def _rnn_branch(h, w_in_bf16, u_col0, g_col0, conv_w, conv_b, w_gate_a, b_gate_a, w_gate_x,
                b_gate_x, lam):
    s, k = h.shape
    nb, bw, _ = w_gate_a.shape
    d = nb * bw
    conv_width = conv_w.shape[0]
    tb = RNN_BLOCK_ROWS
    assert u_col0 % bw == 0 and g_col0 % bw == 0
    vec = pl.BlockSpec((1, bw), lambda c, t: (0, c))
    gate_w = pl.BlockSpec((None, bw, bw), lambda c, t: (c, 0, 0))
    gate_b = pl.BlockSpec((None, 1, bw), lambda c, t: (c, 0, 0))
    scratch_rows = RNN_SUB_ROWS + RNN_HALO_GROUPS * SUBLANES
    block_bytes = (2 * (tb * k * 2 + 2 * k * bw * 2 + tb * bw * 2) + 4 * bw * bw * 2
                   + 3 * scratch_rows * bw * 4 + 12 * RNN_SUB_ROWS * bw * 4)
    return pl.pallas_call(
        functools.partial(_rnn_branch_kernel, conv_width=conv_width),
        out_shape=jax.ShapeDtypeStruct((s, d), jnp.bfloat16),
        grid=(nb, s // tb),
        in_specs=[pl.BlockSpec((tb, k), lambda c, t: (t, 0)),
                  pl.BlockSpec((k, bw), lambda c, t: (0, u_col0 // bw + c)),
                  pl.BlockSpec((k, bw), lambda c, t: (0, g_col0 // bw + c)),
                  pl.BlockSpec((conv_width, bw), lambda c, t: (0, c)), vec,
                  gate_w, gate_b, gate_w, gate_b, vec],
        out_specs=pl.BlockSpec((tb, bw), lambda c, t: (t, c)),
        scratch_shapes=[
            pltpu.VMEM((bw // LANES, scratch_rows, LANES), jnp.float32),
            pltpu.VMEM((bw // LANES, RNN_SUB_ROWS, LANES), jnp.float32),
            pltpu.VMEM((RNN_SUB_ROWS, bw), jnp.float32),
            pltpu.VMEM((1, bw), jnp.float32)],
        compiler_params=pltpu.CompilerParams(
            dimension_semantics=("arbitrary", "arbitrary"),
            vmem_limit_bytes=_vmem_limit(block_bytes)),
        name="rnn_branch",
    )(h, w_in_bf16, w_in_bf16, conv_w, conv_b.reshape(1, d),
      (0.5 * w_gate_a).astype(jnp.bfloat16), b_gate_a.reshape(nb, 1, bw),
      (0.5 * w_gate_x).astype(jnp.bfloat16), b_gate_x.reshape(nb, 1, bw), lam.reshape(1, d))


def _swap_halves(v):
    half = v.shape[1] // 2
    return jnp.concatenate([v[:, half:], v[:, :half]], axis=1)


def _attn_block(first, sink_ref, q_ref, k_prev, k_cur, sg_ref, o_ref, *, n_kv, group, head_dim):
    bq = q_ref.shape[0]
    pair_w = 2 * head_dim
    pairs_per_group = group // 2
    d_kv = n_kv * head_dim

    q_pos = lax.broadcasted_iota(jnp.int32, (bq, 2 * bq), 0)
    k_pos = lax.broadcasted_iota(jnp.int32, (bq, 2 * bq), 1)
    dist = q_pos + bq - k_pos
    valid = (dist >= 0) & (dist < WINDOW) & ((k_pos >= bq) | jnp.logical_not(first))
    lane = lax.broadcasted_iota(jnp.int32, (bq, pair_w), 1)
    lo_q = lane < head_dim
    lane_kv = lax.broadcasted_iota(jnp.int32, (2 * bq, pair_w), 1)
    lo_kv = lane_kv < head_dim
    scale = head_dim ** -0.5
    zero_q = jnp.zeros((bq, pair_w), q_ref.dtype)

    for kv_pair in range(n_kv // 2):
        cols = slice(kv_pair * pair_w, (kv_pair + 1) * pair_w)
        k_two = jnp.concatenate([k_prev[:, cols], k_cur[:, cols]], axis=0)
        v_cols = slice(d_kv + kv_pair * pair_w, d_kv + (kv_pair + 1) * pair_w)
        v_two = jnp.concatenate([k_prev[:, v_cols], k_cur[:, v_cols]], axis=0)
        k_swapped = _swap_halves(k_two)
        v_swapped = _swap_halves(v_two)
        for parity in range(2):
            hk = 2 * kv_pair + parity
            if parity == 0:
                k_dup = jnp.where(lo_kv, k_two, k_swapped)
                v_dup = jnp.where(lo_kv, v_two, v_swapped)
            else:
                k_dup = jnp.where(lo_kv, k_swapped, k_two)
                v_dup = jnp.where(lo_kv, v_swapped, v_two)
            k_dup = k_dup * jnp.asarray(scale, k_dup.dtype)
            q_rows = []
            for p in range(pairs_per_group):
                pair = hk * pairs_per_group + p
                qp = q_ref[:, pair * pair_w:(pair + 1) * pair_w]
                q_rows.append(jnp.where(lo_q, qp, zero_q))
                q_rows.append(jnp.where(lo_q, zero_q, qp))
            q_stack = jnp.concatenate(q_rows, axis=0)
            s_all = lax.dot_general(q_stack, k_dup, (((1,), (1,)), ((), ())),
                                    preferred_element_type=jnp.float32)
            p_rows, inv_rows = [], []
            for g in range(group):
                sink = sink_ref[hk * group + g]
                sc = jnp.where(valid, s_all[g * bq:(g + 1) * bq, :], -jnp.inf)
                m = jnp.maximum(jnp.max(sc, axis=-1, keepdims=True), sink)
                e = jnp.exp(sc - m)
                denom = jnp.sum(e, axis=-1, keepdims=True) + jnp.exp(sink - m)
                p_rows.append(e.astype(v_dup.dtype))
                inv_rows.append(1.0 / denom)
            p_stack = jnp.concatenate(p_rows, axis=0)
            o_all = jnp.dot(p_stack, v_dup, preferred_element_type=jnp.float32)
            for p in range(pairs_per_group):
                pair = hk * pairs_per_group + p
                o_even = o_all[(2 * p) * bq:(2 * p + 1) * bq, :] * inv_rows[2 * p]
                o_odd = o_all[(2 * p + 1) * bq:(2 * p + 2) * bq, :] * inv_rows[2 * p + 1]
                o_pair = jnp.where(lo_q, o_even, o_odd)
                c0 = pair * pair_w
                o_ref[:, c0:c0 + pair_w] = (o_pair * sg_ref[:, c0:c0 + pair_w]).astype(o_ref.dtype)


def _attn_kernel(sink_ref, q_ref, kv_ref, kvp_ref, sg_ref, o_ref, *, n_kv, group, head_dim):
    bq = BLOCK_Q
    n = pl.program_id(0)
    for b in range(q_ref.shape[0] // bq):
        rows = pl.ds(b * bq, bq)
        k_prev = kvp_ref if b == 0 else kv_ref.at[pl.ds((b - 1) * bq, bq), :]
        first = (n == 0) if b == 0 else jnp.bool_(False)
        _attn_block(first, sink_ref, q_ref.at[rows, :], k_prev, kv_ref.at[rows, :],
                    sg_ref.at[rows, :], o_ref.at[rows, :], n_kv=n_kv, group=group,
                    head_dim=head_dim)


def _attention(q, kv, sg_attn, sinks, *, n_kv, head_dim):
    s, d_attn = q.shape
    n_q = d_attn // head_dim
    group = n_q // n_kv
    assert head_dim * 2 == LANES and group % 2 == 0 and n_kv % 2 == 0
    bq = BLOCK_Q
    rows = ATTN_BLOCKS_PER_STEP * bq
    d_kv2 = kv.shape[1]
    block_bytes = 2 * rows * (d_attn * (2 + 4 + 2) + d_kv2 * 2) + 2 * bq * d_kv2 * 2
    return pl.pallas_call(
        functools.partial(_attn_kernel, n_kv=n_kv, group=group, head_dim=head_dim),
        out_shape=jax.ShapeDtypeStruct((s, d_attn), jnp.bfloat16),
        grid=(s // rows,),
        in_specs=[pl.BlockSpec(memory_space=pltpu.SMEM),
                  pl.BlockSpec((rows, d_attn), lambda n: (n, 0)),
                  pl.BlockSpec((rows, d_kv2), lambda n: (n, 0)),
                  pl.BlockSpec((bq, d_kv2),
                               lambda n: (jnp.maximum(n * ATTN_BLOCKS_PER_STEP - 1, 0), 0)),
                  pl.BlockSpec((rows, d_attn), lambda n: (n, 0))],
        out_specs=pl.BlockSpec((rows, d_attn), lambda n: (n, 0)),
        compiler_params=pltpu.CompilerParams(
            dimension_semantics=("arbitrary",),
            vmem_limit_bytes=_vmem_limit(block_bytes + 16 * 1024 * 1024)),
        name="swa_attn",
    )(sinks, q, kv, kv, sg_attn)


def kernel(x, c, w_ada, b_ada, norm_gain, w_in, conv_w, conv_b, w_gate_a, b_gate_a, w_gate_x,
           b_gate_x, lru_lambda, attn_sinks, w_rnn_proj, w_attn_proj, w_out, final_norm_gain):
    bsz, seq, d = x.shape
    depth = w_in.shape[0]
    d_rnn = w_rnn_proj.shape[1]
    d_attn = w_attn_proj.shape[1]
    n_q = attn_sinks.shape[1]
    head_dim = d_attn // n_q
    d_kv = (w_in.shape[2] - 2 * d_rnn - 2 * d_attn - 2 * d) // 2
    n_kv = d_kv // head_dim
    assert bsz == 1

    xs = x.reshape(bsz * seq, d)
    bf = jnp.bfloat16
    for l in range(depth):
        mod = _modulation(c, w_ada[l], b_ada[l])
        shift, scale, gate = mod[:, :d], mod[:, d:2 * d], mod[:, 2 * d:]
        h = _norm_mod(xs, norm_gain[l].reshape(1, d), scale, shift)

        w = w_in[l].astype(bf)
        proj = functools.partial(_matmul, h, w)
        y_rnn = _rnn_branch(h, w, 0, d_rnn, conv_w[l], conv_b[l], w_gate_a[l], b_gate_a[l],
                            w_gate_x[l], b_gate_x[l], lru_lambda[l])
        o = 2 * d_rnn
        q = proj(col_offset=o, n_cols=d_attn, out_dtype=bf, name="proj_q")
        o += d_attn
        kv = proj(col_offset=o, n_cols=2 * d_kv, out_dtype=bf, name="proj_kv")
        o += 2 * d_kv
        sg_attn = proj(col_offset=o, n_cols=d_attn, out_dtype=jnp.float32, epilogue=_ep_silu,
                       name="proj_g_attn")
        o += d_attn
        sm_rnn = proj(col_offset=o, n_cols=d, out_dtype=jnp.float32, epilogue=_ep_sigmoid,
                      name="proj_m_rnn")
        o += d
        sm_attn = proj(col_offset=o, n_cols=d, out_dtype=jnp.float32, epilogue=_ep_sigmoid,
                       name="proj_m_attn")

        y_attn = _attention(q, kv, sg_attn, attn_sinks[l], n_kv=n_kv, head_dim=head_dim)

        merged = _merge(y_rnn, y_attn, w_rnn_proj[l].astype(bf), w_attn_proj[l].astype(bf),
                        sm_rnn, sm_attn)
        if l + 1 < depth:
            xs = _matmul(merged, w_out[l].astype(bf), col_offset=0, n_cols=d,
                         out_dtype=jnp.float32, epilogue=_ep_residual, tiles=(xs,),
                         rows=(gate,), name="out_proj")
        else:
            xs = _out_norm(merged, w_out[l].astype(bf), xs, gate, final_norm_gain.reshape(1, d))
    return xs.reshape(bsz, seq, d)
```

```python
import functools
from typing import NamedTuple

import jax
import jax.numpy as jnp
from jax import lax
from jax.experimental import pallas as pl
from jax.experimental.pallas import tpu as pltpu

EPS = 1e-6
LRU_C = 8.0
WINDOW = 128
BLOCK_Q = 128
ATTN_BLOCKS_PER_STEP = 2

V7X_VMEM_BYTES = 64 * 1024 * 1024
VMEM_HEADROOM_BYTES = 6 * 1024 * 1024
LANES = 128
SUBLANES = 8


def _vmem_limit(block_bytes):
    need = int(block_bytes) + VMEM_HEADROOM_BYTES
    assert need <= V7X_VMEM_BYTES - 2 * 1024 * 1024, need
    return need


def _sigmoid(v):
    return 0.5 * jnp.tanh(0.5 * v) + 0.5


def _silu(v):
    return v * _sigmoid(v)


def _mod_kernel(c_ref, w_ref, b_ref, o_ref):
    c_act = _silu(c_ref[...])
    lhs = jnp.broadcast_to(c_act, (SUBLANES, c_act.shape[1]))
    acc = jnp.dot(lhs, w_ref[...], preferred_element_type=jnp.float32,
                  precision=lax.Precision.HIGHEST)
    o_ref[...] = acc[0:1, :] + b_ref[...]


def _modulation(c, w_ada, b_ada):
    d, n = w_ada.shape
    tn = 512
    return pl.pallas_call(
        _mod_kernel,
        out_shape=jax.ShapeDtypeStruct((1, n), jnp.float32),
        grid=(n // tn,),
        in_specs=[pl.BlockSpec((1, d), lambda j: (0, 0)),
                  pl.BlockSpec((d, tn), lambda j: (0, j)),
                  pl.BlockSpec((1, tn), lambda j: (0, j))],
        out_specs=pl.BlockSpec((1, tn), lambda j: (0, j)),
        compiler_params=pltpu.CompilerParams(
            dimension_semantics=("arbitrary",),
            vmem_limit_bytes=_vmem_limit(2 * d * tn * 4)),
        name="adaln_mod",
    )(c, w_ada, b_ada.reshape(1, n))


def _norm_mod_kernel(x_ref, gain_ref, scale_ref, shift_ref, o_ref):
    x = x_ref[...]
    ms = jnp.mean(x * x, axis=-1, keepdims=True)
    g = gain_ref[...] * (1.0 + scale_ref[...])
    o_ref[...] = (x * lax.rsqrt(ms + EPS) * g + shift_ref[...]).astype(o_ref.dtype)


def _norm_mod(x2, gain, scale, shift):
    s, d = x2.shape
    tm = 256
    row = pl.BlockSpec((1, d), lambda i: (0, 0))
    return pl.pallas_call(
        _norm_mod_kernel,
        out_shape=jax.ShapeDtypeStruct((s, d), jnp.bfloat16),
        grid=(s // tm,),
        in_specs=[pl.BlockSpec((tm, d), lambda i: (i, 0)), row, row, row],
        out_specs=pl.BlockSpec((tm, d), lambda i: (i, 0)),
        compiler_params=pltpu.CompilerParams(
            dimension_semantics=("arbitrary",),
            vmem_limit_bytes=_vmem_limit(2 * tm * d * (4 + 2) + 2 * tm * d * 4)),
        name="norm_mod",
    )(x2, gain, scale, shift)


CAST_BLOCK_COLS = 1024


class CastJob(NamedTuple):
    src: jax.Array
    col0: int
    cols: int


def _cast_specs(job, grid):
    g0, g1 = grid
    n_steps = g0 * g1
    rows = job.src.shape[0]
    cw = min(job.cols, CAST_BLOCK_COLS)
    n_cb = job.cols // cw
    rb = rows * n_cb // n_steps
    assert rb * n_steps == rows * n_cb and rb % 16 == 0 and job.col0 % cw == 0
    col_blk0 = job.col0 // cw

    def blk(i, j):
        s = i * g1 + j
        return s // n_cb, s % n_cb

    in_spec = pl.BlockSpec((rb, cw), lambda i, j: (blk(i, j)[0], col_blk0 + blk(i, j)[1]))
    out_spec = pl.BlockSpec((rb, cw), lambda i, j: blk(i, j))
    out_shape = jax.ShapeDtypeStruct((rows, job.cols), jnp.bfloat16)
    return in_spec, out_spec, out_shape, 2 * rb * cw * (4 + 2)


def _run_cast(src_ref, dst_ref):
    dst_ref[...] = src_ref[...].astype(dst_ref.dtype)


def _ep_identity(acc):
    return acc


def _ep_silu(acc):
    return _silu(acc)


def _ep_sigmoid(acc):
    return _sigmoid(acc)


def _ep_residual(acc, x, gate):
    return x + gate * acc


def _matmul_kernel(*refs, epilogue, n_tiles, n_rows, has_cast):
    lhs_ref, rhs_ref = refs[0], refs[1]
    n_in = 2 + n_tiles + n_rows
    tile_refs = refs[2:2 + n_tiles]
    row_refs = refs[2 + n_tiles:n_in]
    o_ref = refs[n_in + has_cast]
    if has_cast:
        _run_cast(refs[n_in], refs[n_in + 2])
    acc = jnp.dot(lhs_ref[...], rhs_ref[...], preferred_element_type=jnp.float32)
    extras = [r[...] for r in tile_refs] + [r[...] for r in row_refs]
    o_ref[...] = epilogue(acc, *extras).astype(o_ref.dtype)


def _matmul(lhs, rhs, *, out_dtype, epilogue=_ep_identity, tiles=(), rows=(), tn=1024,
            cast=None, name):
    m, k = lhs.shape
    n_cols = rhs.shape[1]
    tm = 512 if tiles else 1024
    assert rhs.shape[0] == k and n_cols % tn == 0 and m % tm == 0
    grid = (m // tm, n_cols // tn)
    in_specs = [pl.BlockSpec((tm, k), lambda i, j: (i, 0)),
                pl.BlockSpec((k, tn), lambda i, j: (0, j))]
    in_specs += [pl.BlockSpec((tm, tn), lambda i, j: (i, j)) for _ in tiles]
    in_specs += [pl.BlockSpec((1, tn), lambda i, j: (0, j)) for _ in rows]
    out_bytes = jnp.dtype(out_dtype).itemsize
    tile_bytes = sum(t.dtype.itemsize for t in tiles)
    block_bytes = 2 * (tm * k * lhs.dtype.itemsize + k * tn * rhs.dtype.itemsize
                       + tm * tn * (out_bytes + tile_bytes)) + 2 * tm * tn * 4
    out_shape = jax.ShapeDtypeStruct((m, n_cols), out_dtype)
    out_spec = pl.BlockSpec((tm, tn), lambda i, j: (i, j))
    args = (lhs, rhs, *tiles, *rows)
    if cast is not None:
        c_in, c_out, c_shape, c_bytes = _cast_specs(cast, grid)
        in_specs.append(c_in)
        out_shape, out_spec = (out_shape, c_shape), (out_spec, c_out)
        args += (cast.src,)
        block_bytes += c_bytes
    return pl.pallas_call(
        functools.partial(_matmul_kernel, epilogue=epilogue, n_tiles=len(tiles),
                          n_rows=len(rows), has_cast=cast is not None),
        out_shape=out_shape,
        grid=grid,
        in_specs=in_specs,
        out_specs=out_spec,
        compiler_params=pltpu.CompilerParams(
            dimension_semantics=("arbitrary", "arbitrary"),
            vmem_limit_bytes=_vmem_limit(block_bytes)),
        name=name,
    )(*args)


def _merge_kernel(yr_ref, ya_ref, wr_ref, wa_ref, smr_ref, sma_ref, cast_src, o_ref, cast_dst):
    _run_cast(cast_src, cast_dst)
    p_rnn = jnp.dot(yr_ref[...], wr_ref[...], preferred_element_type=jnp.float32)
    p_attn = jnp.dot(ya_ref[...], wa_ref[...], preferred_element_type=jnp.float32)
    o_ref[...] = (smr_ref[...] * p_rnn + sma_ref[...] * p_attn).astype(o_ref.dtype)


def _merge(y_rnn, y_attn, w_rnn, w_attn, sm_rnn, sm_attn, cast):
    m, k = y_rnn.shape
    n = w_rnn.shape[1]
    tm, tn = 512, 512
    lhs = pl.BlockSpec((tm, k), lambda i, j: (i, 0))
    rhs = pl.BlockSpec((k, tn), lambda i, j: (0, j))
    tile = pl.BlockSpec((tm, tn), lambda i, j: (i, j))
    grid = (m // tm, n // tn)
    c_in, c_out, c_shape, c_bytes = _cast_specs(cast, grid)
    block_bytes = (2 * (2 * tm * k * 2 + 2 * k * tn * 2 + tm * tn * (4 + 4 + 2)) + 3 * tm * tn * 4
                   + c_bytes)
    return pl.pallas_call(
        _merge_kernel,
        out_shape=(jax.ShapeDtypeStruct((m, n), jnp.bfloat16), c_shape),
        grid=grid,
        in_specs=[lhs, lhs, rhs, rhs, tile, tile, c_in],
        out_specs=(tile, c_out),
        compiler_params=pltpu.CompilerParams(
            dimension_semantics=("arbitrary", "arbitrary"),
            vmem_limit_bytes=_vmem_limit(block_bytes)),
        name="merge_proj",
    )(y_rnn, y_attn, w_rnn, w_attn, sm_rnn, sm_attn, cast.src)


def _out_norm_kernel(m_ref, w_ref, x_ref, gate_ref, gain_ref, o_ref, res_ref):
    j = pl.program_id(1)
    n_j = pl.num_programs(1)
    tn = w_ref.shape[1]
    acc = jnp.dot(m_ref[...], w_ref[...], preferred_element_type=jnp.float32)
    res_ref[j] = x_ref[...] + gate_ref[...] * acc

    @pl.when(j == n_j - 1)
    def _():
        n_tiles = res_ref.shape[0]
        ss = jnp.sum(res_ref[0] * res_ref[0], axis=-1, keepdims=True)
        for t in range(1, n_tiles):
            ss = ss + jnp.sum(res_ref[t] * res_ref[t], axis=-1, keepdims=True)
        inv = lax.rsqrt(ss / (n_tiles * tn) + EPS)
        for t in range(n_tiles):
            o_ref[:, t * tn:(t + 1) * tn] = res_ref[t] * inv * gain_ref[:, t * tn:(t + 1) * tn]


def _out_norm(merged, w, x2, gate, gain):
    m, k = merged.shape
    n = w.shape[1]
    tm, tn = 512, 512
    block_bytes = (2 * (tm * k * 2 + k * tn * 2 + tm * tn * 4 + tm * n * 4) + tm * n * 4
                   + 3 * tm * tn * 4)
    return pl.pallas_call(
        _out_norm_kernel,
        out_shape=jax.ShapeDtypeStruct((m, n), jnp.float32),
        grid=(m // tm, n // tn),
        in_specs=[pl.BlockSpec((tm, k), lambda i, j: (i, 0)),
                  pl.BlockSpec((k, tn), lambda i, j: (0, j)),
                  pl.BlockSpec((tm, tn), lambda i, j: (i, j)),
                  pl.BlockSpec((1, tn), lambda i, j: (0, j)),
                  pl.BlockSpec((1, n), lambda i, j: (0, 0))],
        out_specs=pl.BlockSpec((tm, n), lambda i, j: (i, 0)),
        scratch_shapes=[pltpu.VMEM((n // tn, tm, tn), jnp.float32)],
        compiler_params=pltpu.CompilerParams(
            dimension_semantics=("arbitrary", "arbitrary"),
            vmem_limit_bytes=_vmem_limit(block_bytes)),
        name="out_norm",
    )(merged, w, x2, gate, gain)


RNN_SUB_ROWS = 256
RNN_BLOCK_ROWS = 2048
RNN_HALO_GROUPS = 4
SQRT_FLOOR = 1e-30


def _slabs(ref, rows, n_slabs):
    return jnp.concatenate([ref[l, rows, :] for l in range(n_slabs)], axis=1)


def _rglru_kernel(u_ref, sg_ref, cw_ref, cb_ref, wa_ref, ba_ref, wx_ref, bx_ref, lam_ref,
                  cast_src, y_ref, cast_dst, uperm, hbuf, pbuf, carry_ref, *, conv_width):
    ts = RNN_SUB_ROWS
    tc = ts // SUBLANES
    width = u_ref.shape[1]
    n_slabs = width // LANES
    data0 = RNN_HALO_GROUPS * SUBLANES
    taps = conv_width - 1
    assert taps < RNN_HALO_GROUPS and taps <= tc
    _run_cast(cast_src, cast_dst)

    @pl.when(pl.program_id(1) == 0)
    def _():
        uperm[...] = jnp.zeros_like(uperm)
        carry_ref[...] = jnp.zeros_like(carry_ref)

    neg = -lam_ref[...]
    softplus_neg = jnp.maximum(neg, 0.0) + jnp.log1p(jnp.exp(-jnp.abs(neg)))
    half_decay = (-0.5 * LRU_C) * softplus_neg
    half_ba = 0.5 * ba_ref[...]
    half_bx = 0.5 * bx_ref[...]
    cw = cw_ref[...]
    sub = lax.broadcasted_iota(jnp.int32, (SUBLANES, width), 0)

    def group(j):
        return pl.ds(data0 + j * SUBLANES, SUBLANES)

    def sub_tile(s, _):
        r0 = pl.multiple_of(s * ts, ts)
        prev = [_slabs(uperm, group(tc - k), n_slabs) for k in range(1, taps + 1)]
        for c in range(SUBLANES):
            for l in range(n_slabs):
                uperm[l, pl.ds(data0 + c, tc, stride=SUBLANES), :] = (
                    u_ref[pl.ds(r0 + c * tc, tc), l * LANES:(l + 1) * LANES])
        for k in range(1, taps + 1):
            cur = _slabs(uperm, group(tc - k), n_slabs)
            halo = jnp.where(sub == 0, pltpu.roll(prev[k - 1], 1, axis=0),
                             pltpu.roll(cur, 1, axis=0))
            for l in range(n_slabs):
                uperm[l, group(-k), :] = halo[:, l * LANES:(l + 1) * LANES]
        u = cb_ref[...]
        for k in range(conv_width):
            first = data0 - (taps - k) * SUBLANES
            u = u + cw[k:k + 1, :] * _slabs(uperm, pl.ds(first, ts), n_slabs)
        ub = u.astype(jnp.bfloat16)
        t_r = jnp.tanh(jnp.dot(ub, wa_ref[...], preferred_element_type=jnp.float32) + half_ba)
        t_i = jnp.tanh(jnp.dot(ub, wx_ref[...], preferred_element_type=jnp.float32) + half_bx)
        log_a = half_decay * t_r + half_decay
        a = jnp.exp(log_a)
        half_u = 0.5 * u
        gated_u = half_u * t_i + half_u
        one_m_a2 = 1.0 - a * a
        b = (one_m_a2 * lax.rsqrt(jnp.maximum(one_m_a2, SQRT_FLOOR))) * gated_u

        h_loc = b[0:SUBLANES]
        p_cum = a[0:SUBLANES]
        for j in range(tc):
            if j > 0:
                aj = a[j * SUBLANES:(j + 1) * SUBLANES]
                h_loc = aj * h_loc + b[j * SUBLANES:(j + 1) * SUBLANES]
                p_cum = aj * p_cum
            pbuf[j * SUBLANES:(j + 1) * SUBLANES, :] = p_cum
            for l in range(n_slabs):
                hbuf[l, j * SUBLANES:(j + 1) * SUBLANES, :] = h_loc[:, l * LANES:(l + 1) * LANES]

        a_inc, b_inc = p_cum, h_loc
        shift = 1
        while shift < SUBLANES:
            keep = sub >= shift
            a_prev = jnp.where(keep, pltpu.roll(a_inc, shift, axis=0), 1.0)
            b_prev = jnp.where(keep, pltpu.roll(b_inc, shift, axis=0), 0.0)
            b_inc = a_inc * b_prev + b_inc
            a_inc = a_inc * a_prev
            shift *= 2
        h_in = jnp.broadcast_to(carry_ref[...], (SUBLANES, width))
        chunk_end = b_inc + a_inc * h_in
        entry = jnp.where(sub == 0, h_in, pltpu.roll(chunk_end, 1, axis=0))
        carry_ref[...] = chunk_end[SUBLANES - 1:SUBLANES, :]

        for j in range(tc):
            rows = slice(j * SUBLANES, (j + 1) * SUBLANES)
            h = _slabs(hbuf, rows, n_slabs) + pbuf[rows, :] * entry
            for l in range(n_slabs):
                hbuf[l, rows, :] = h[:, l * LANES:(l + 1) * LANES]
        for c in range(SUBLANES):
            rows = pl.ds(r0 + c * tc, tc)
            h_nat = _slabs(hbuf, pl.ds(c, tc, stride=SUBLANES), n_slabs)
            y_ref[rows, :] = (h_nat * sg_ref[rows, :]).astype(y_ref.dtype)
        return 0

    lax.fori_loop(0, u_ref.shape[0] // ts, sub_tile, 0, unroll=2)


def _rglru(u_rnn, sg_rnn, conv_w, conv_b, w_gate_a, b_gate_a, w_gate_x, b_gate_x, lam, cast):
    s, d = u_rnn.shape
    nb, bw, _ = w_gate_a.shape
    conv_width = conv_w.shape[0]
    tb = RNN_BLOCK_ROWS
    seq = pl.BlockSpec((tb, bw), lambda c, t: (t, c))
    vec = pl.BlockSpec((1, bw), lambda c, t: (0, c))
    gate_w = pl.BlockSpec((None, bw, bw), lambda c, t: (c, 0, 0))
    gate_b = pl.BlockSpec((None, 1, bw), lambda c, t: (c, 0, 0))
    block_bytes = (2 * tb * bw * (4 + 4 + 2) + 4 * bw * bw * 2
                   + 3 * (RNN_SUB_ROWS + RNN_HALO_GROUPS * SUBLANES) * bw * 4)
    grid = (nb, s // tb)
    c_in, c_out, c_shape, c_bytes = _cast_specs(cast, grid)
    return pl.pallas_call(
        functools.partial(_rglru_kernel, conv_width=conv_width),
        out_shape=(jax.ShapeDtypeStruct((s, d), jnp.bfloat16), c_shape),
        grid=grid,
        in_specs=[seq, seq,
                  pl.BlockSpec((conv_width, bw), lambda c, t: (0, c)), vec,
                  gate_w, gate_b, gate_w, gate_b, vec, c_in],
        out_specs=(seq, c_out),
        scratch_shapes=[
            pltpu.VMEM((bw // LANES, RNN_SUB_ROWS + RNN_HALO_GROUPS * SUBLANES, LANES),
                       jnp.float32),
            pltpu.VMEM((bw // LANES, RNN_SUB_ROWS, LANES), jnp.float32),
            pltpu.VMEM((RNN_SUB_ROWS, bw), jnp.float32),
            pltpu.VMEM((1, bw), jnp.float32)],
        compiler_params=pltpu.CompilerParams(
            dimension_semantics=("arbitrary", "arbitrary"),
            vmem_limit_bytes=_vmem_limit(block_bytes + c_bytes + 16 * 1024 * 1024)),
        name="rglru",
    )(u_rnn, sg_rnn, conv_w, conv_b.reshape(1, d),
      (0.5 * w_gate_a).astype(jnp.bfloat16), b_gate_a.reshape(nb, 1, bw),
      (0.5 * w_gate_x).astype(jnp.bfloat16), b_gate_x.reshape(nb, 1, bw), lam.reshape(1, d),
      cast.src)


def _swap_halves(v):
    half = v.shape[1] // 2
    return jnp.concatenate([v[:, half:], v[:, :half]], axis=1)


def _attn_block(first, sink_ref, q_ref, k_prev, k_cur, sg_ref, o_ref, *, n_kv, group, head_dim):
    bq = q_ref.shape[0]
    pair_w = 2 * head_dim
    pairs_per_group = group // 2
    d_kv = n_kv * head_dim

    q_pos = lax.broadcasted_iota(jnp.int32, (bq, 2 * bq), 0)
    k_pos = lax.broadcasted_iota(jnp.int32, (bq, 2 * bq), 1)
    dist = q_pos + bq - k_pos
    valid = (dist >= 0) & (dist < WINDOW) & ((k_pos >= bq) | jnp.logical_not(first))
    lane = lax.broadcasted_iota(jnp.int32, (bq, pair_w), 1)
    lo_q = lane < head_dim
    lane_kv = lax.broadcasted_iota(jnp.int32, (2 * bq, pair_w), 1)
    lo_kv = lane_kv < head_dim
    scale = head_dim ** -0.5
    zero_q = jnp.zeros((bq, pair_w), q_ref.dtype)

    for kv_pair in range(n_kv // 2):
        cols = slice(kv_pair * pair_w, (kv_pair + 1) * pair_w)
        k_two = jnp.concatenate([k_prev[:, cols], k_cur[:, cols]], axis=0)
        v_cols = slice(d_kv + kv_pair * pair_w, d_kv + (kv_pair + 1) * pair_w)
        v_two = jnp.concatenate([k_prev[:, v_cols], k_cur[:, v_cols]], axis=0)
        k_swapped = _swap_halves(k_two)
        v_swapped = _swap_halves(v_two)
        for parity in range(2):
            hk = 2 * kv_pair + parity
            if parity == 0:
                k_dup = jnp.where(lo_kv, k_two, k_swapped)
                v_dup = jnp.where(lo_kv, v_two, v_swapped)
            else:
                k_dup = jnp.where(lo_kv, k_swapped, k_two)
                v_dup = jnp.where(lo_kv, v_swapped, v_two)
            k_dup = k_dup * jnp.asarray(scale, k_dup.dtype)
            q_rows = []
            for p in range(pairs_per_group):
                pair = hk * pairs_per_group + p
                qp = q_ref[:, pair * pair_w:(pair + 1) * pair_w]
                q_rows.append(jnp.where(lo_q, qp, zero_q))
                q_rows.append(jnp.where(lo_q, zero_q, qp))
            q_stack = jnp.concatenate(q_rows, axis=0)
            s_all = lax.dot_general(q_stack, k_dup, (((1,), (1,)), ((), ())),
                                    preferred_element_type=jnp.float32)
            p_rows, inv_rows = [], []
            for g in range(group):
                sink = sink_ref[hk * group + g]
                sc = jnp.where(valid, s_all[g * bq:(g + 1) * bq, :], -jnp.inf)
                m = jnp.maximum(jnp.max(sc, axis=-1, keepdims=True), sink)
                e = jnp.exp(sc - m)
                denom = jnp.sum(e, axis=-1, keepdims=True) + jnp.exp(sink - m)
                p_rows.append(e.astype(v_dup.dtype))
                inv_rows.append(1.0 / denom)
            p_stack = jnp.concatenate(p_rows, axis=0)
            o_all = jnp.dot(p_stack, v_dup, preferred_element_type=jnp.float32)
            for p in range(pairs_per_group):
                pair = hk * pairs_per_group + p
                o_even = o_all[(2 * p) * bq:(2 * p + 1) * bq, :] * inv_rows[2 * p]
                o_odd = o_all[(2 * p + 1) * bq:(2 * p + 2) * bq, :] * inv_rows[2 * p + 1]
                o_pair = jnp.where(lo_q, o_even, o_odd)
                c0 = pair * pair_w
                o_ref[:, c0:c0 + pair_w] = (o_pair * sg_ref[:, c0:c0 + pair_w]).astype(o_ref.dtype)


def _attn_kernel(sink_ref, q_ref, kv_ref, kvp_ref, sg_ref, o_ref, *, n_kv, group, head_dim):
    bq = BLOCK_Q
    n = pl.program_id(0)
    for b in range(q_ref.shape[0] // bq):
        rows = pl.ds(b * bq, bq)
        k_prev = kvp_ref if b == 0 else kv_ref.at[pl.ds((b - 1) * bq, bq), :]
        first = (n == 0) if b == 0 else jnp.bool_(False)
        _attn_block(first, sink_ref, q_ref.at[rows, :], k_prev, kv_ref.at[rows, :],
                    sg_ref.at[rows, :], o_ref.at[rows, :], n_kv=n_kv, group=group,
                    head_dim=head_dim)


def _attention(q, kv, sg_attn, sinks, *, n_kv, head_dim):
    s, d_attn = q.shape
    n_q = d_attn // head_dim
    group = n_q // n_kv
    assert head_dim * 2 == LANES and group % 2 == 0 and n_kv % 2 == 0
    bq = BLOCK_Q
    rows = ATTN_BLOCKS_PER_STEP * bq
    d_kv2 = kv.shape[1]
    block_bytes = 2 * rows * (d_attn * (2 + 4 + 2) + d_kv2 * 2) + 2 * bq * d_kv2 * 2
    return pl.pallas_call(
        functools.partial(_attn_kernel, n_kv=n_kv, group=group, head_dim=head_dim),
        out_shape=jax.ShapeDtypeStruct((s, d_attn), jnp.bfloat16),
        grid=(s // rows,),
        in_specs=[pl.BlockSpec(memory_space=pltpu.SMEM),
                  pl.BlockSpec((rows, d_attn), lambda n: (n, 0)),
                  pl.BlockSpec((rows, d_kv2), lambda n: (n, 0)),
                  pl.BlockSpec((bq, d_kv2),
                               lambda n: (jnp.maximum(n * ATTN_BLOCKS_PER_STEP - 1, 0), 0)),
                  pl.BlockSpec((rows, d_attn), lambda n: (n, 0))],
        out_specs=pl.BlockSpec((rows, d_attn), lambda n: (n, 0)),
        compiler_params=pltpu.CompilerParams(
            dimension_semantics=("arbitrary",),
            vmem_limit_bytes=_vmem_limit(block_bytes + 16 * 1024 * 1024)),
        name="swa_attn",
    )(sinks, q, kv, kv, sg_attn)


def kernel(x, c, w_ada, b_ada, norm_gain, w_in, conv_w, conv_b, w_gate_a, b_gate_a, w_gate_x,
           b_gate_x, lru_lambda, attn_sinks, w_rnn_proj, w_attn_proj, w_out, final_norm_gain):
    bsz, seq, d = x.shape
    depth = w_in.shape[0]
    d_rnn = w_rnn_proj.shape[1]
    d_attn = w_attn_proj.shape[1]
    n_q = attn_sinks.shape[1]
    head_dim = d_attn // n_q
    d_kv = (w_in.shape[2] - 2 * d_rnn - 2 * d_attn - 2 * d) // 2
    n_kv = d_kv // head_dim
    assert bsz == 1

    xs = x.reshape(bsz * seq, d)
    bf = jnp.bfloat16
    for l in range(depth):
        mod = _modulation(c, w_ada[l], b_ada[l])
        shift, scale, gate = mod[:, :d], mod[:, d:2 * d], mod[:, 2 * d:]
        h = _norm_mod(xs, norm_gain[l].reshape(1, d), scale, shift)

        wl = w_in[l]
        c_u, c_g, c_q, c_kv = 0, d_rnn, 2 * d_rnn, 2 * d_rnn + d_attn
        c_ga = c_kv + 2 * d_kv
        c_mr, c_ma = c_ga + d_attn, c_ga + d_attn + d
        proj = functools.partial(_matmul, h)
        w_kv = wl[:, c_kv:c_ga].astype(bf)
        w_u = wl[:, c_u:c_g].astype(bf)
        kv = proj(w_kv, out_dtype=bf, name="proj_kv")
        u_rnn, w_g = proj(w_u, out_dtype=jnp.float32, cast=CastJob(wl, c_g, d_rnn),
                          name="proj_u")
        sg_rnn, w_q = proj(w_g, out_dtype=jnp.float32, epilogue=_ep_silu,
                           cast=CastJob(wl, c_q, d_attn), name="proj_g_rnn")
        q, w_ga = proj(w_q, out_dtype=bf, cast=CastJob(wl, c_ga, d_attn), name="proj_q")
        sg_attn, w_mr = proj(w_ga, out_dtype=jnp.float32, epilogue=_ep_silu,
                             cast=CastJob(wl, c_mr, d), name="proj_g_attn")
        sm_rnn, w_ma = proj(w_mr, out_dtype=jnp.float32, epilogue=_ep_sigmoid,
                            cast=CastJob(wl, c_ma, d), name="proj_m_rnn")
        sm_attn, w_rp = proj(w_ma, out_dtype=jnp.float32, epilogue=_ep_sigmoid,
                             cast=CastJob(w_rnn_proj[l], 0, d), name="proj_m_attn")

        y_rnn, w_ap = _rglru(u_rnn, sg_rnn, conv_w[l], conv_b[l], w_gate_a[l], b_gate_a[l],
                             w_gate_x[l], b_gate_x[l], lru_lambda[l],
                             CastJob(w_attn_proj[l], 0, d))
        y_attn = _attention(q, kv, sg_attn, attn_sinks[l], n_kv=n_kv, head_dim=head_dim)

        merged, w_o = _merge(y_rnn, y_attn, w_rp, w_ap, sm_rnn, sm_attn,
                             CastJob(w_out[l], 0, d))
        if l + 1 < depth:
            xs = _matmul(merged, w_o, out_dtype=jnp.float32, epilogue=_ep_residual,
                         tiles=(xs,), rows=(gate,), name="out_proj")
        else:
            xs = _out_norm(merged, w_o, xs, gate, final_norm_gain.reshape(1, d))
    return xs.reshape(bsz, seq, d)
```

```python
import functools
from typing import NamedTuple

import jax
import jax.numpy as jnp
from jax import lax
from jax.experimental import pallas as pl
from jax.experimental.pallas import tpu as pltpu

EPS = 1e-6
LRU_C = 8.0
WINDOW = 128
BLOCK_Q = 128
ATTN_BLOCKS_PER_STEP = 2

V7X_VMEM_BYTES = 64 * 1024 * 1024
VMEM_HEADROOM_BYTES = 6 * 1024 * 1024
LANES = 128
SUBLANES = 8


def _vmem_limit(block_bytes):
    need = int(block_bytes) + VMEM_HEADROOM_BYTES
    assert need <= V7X_VMEM_BYTES - 2 * 1024 * 1024, need
    return need


def _sigmoid(v):
    return 0.5 * jnp.tanh(0.5 * v) + 0.5


def _silu(v):
    return v * _sigmoid(v)


def _mod_kernel(c_ref, w_ref, b_ref, o_ref):
    c_act = _silu(c_ref[...])
    lhs = jnp.broadcast_to(c_act, (SUBLANES, c_act.shape[1]))
    acc = jnp.dot(lhs, w_ref[...], preferred_element_type=jnp.float32,
                  precision=lax.Precision.HIGHEST)
    o_ref[...] = acc[0:1, :] + b_ref[...]


def _modulation(c, w_ada, b_ada):
    d, n = w_ada.shape
    tn = 512
    return pl.pallas_call(
        _mod_kernel,
        out_shape=jax.ShapeDtypeStruct((1, n), jnp.float32),
        grid=(n // tn,),
        in_specs=[pl.BlockSpec((1, d), lambda j: (0, 0)),
                  pl.BlockSpec((d, tn), lambda j: (0, j)),
                  pl.BlockSpec((1, tn), lambda j: (0, j))],
        out_specs=pl.BlockSpec((1, tn), lambda j: (0, j)),
        compiler_params=pltpu.CompilerParams(
            dimension_semantics=("arbitrary",),
            vmem_limit_bytes=_vmem_limit(2 * d * tn * 4)),
        name="adaln_mod",
    )(c, w_ada, b_ada.reshape(1, n))


def _norm_mod_kernel(x_ref, gain_ref, scale_ref, shift_ref, o_ref):
    x = x_ref[...]
    ms = jnp.mean(x * x, axis=-1, keepdims=True)
    g = gain_ref[...] * (1.0 + scale_ref[...])
    o_ref[...] = (x * lax.rsqrt(ms + EPS) * g + shift_ref[...]).astype(o_ref.dtype)


def _norm_mod(x2, gain, scale, shift):
    s, d = x2.shape
    tm = 256
    row = pl.BlockSpec((1, d), lambda i: (0, 0))
    return pl.pallas_call(
        _norm_mod_kernel,
        out_shape=jax.ShapeDtypeStruct((s, d), jnp.bfloat16),
        grid=(s // tm,),
        in_specs=[pl.BlockSpec((tm, d), lambda i: (i, 0)), row, row, row],
        out_specs=pl.BlockSpec((tm, d), lambda i: (i, 0)),
        compiler_params=pltpu.CompilerParams(
            dimension_semantics=("arbitrary",),
            vmem_limit_bytes=_vmem_limit(2 * tm * d * (4 + 2) + 2 * tm * d * 4)),
        name="norm_mod",
    )(x2, gain, scale, shift)


CAST_BLOCK_COLS = 1024


class CastJob(NamedTuple):
    src: jax.Array
    col0: int
    cols: int


def _cast_specs(job, grid):
    g0, g1 = grid
    n_steps = g0 * g1
    rows = job.src.shape[0]
    cw = min(job.cols, CAST_BLOCK_COLS)
    n_cb = job.cols // cw
    rb = rows * n_cb // n_steps
    assert rb * n_steps == rows * n_cb and rb % 16 == 0 and job.col0 % cw == 0
    col_blk0 = job.col0 // cw

    def blk(i, j):
        s = i * g1 + j
        return s // n_cb, s % n_cb

    in_spec = pl.BlockSpec((rb, cw), lambda i, j: (blk(i, j)[0], col_blk0 + blk(i, j)[1]))
    out_spec = pl.BlockSpec((rb, cw), lambda i, j: blk(i, j))
    out_shape = jax.ShapeDtypeStruct((rows, job.cols), jnp.bfloat16)
    return in_spec, out_spec, out_shape, 2 * rb * cw * (4 + 2)


def _run_cast(src_ref, dst_ref):
    dst_ref[...] = src_ref[...].astype(dst_ref.dtype)


def _cast_cols(job):
    rows = job.src.shape[0]
    rb = 512
    cw = min(job.cols, CAST_BLOCK_COLS)
    assert rows % rb == 0 and job.cols % cw == 0 and job.col0 % cw == 0
    col_blk0 = job.col0 // cw
    return pl.pallas_call(
        _run_cast,
        out_shape=jax.ShapeDtypeStruct((rows, job.cols), jnp.bfloat16),
        grid=(rows // rb, job.cols // cw),
        in_specs=[pl.BlockSpec((rb, cw), lambda i, j: (i, col_blk0 + j))],
        out_specs=pl.BlockSpec((rb, cw), lambda i, j: (i, j)),
        compiler_params=pltpu.CompilerParams(
            dimension_semantics=("arbitrary", "arbitrary"),
            vmem_limit_bytes=_vmem_limit(2 * rb * cw * (4 + 2))),
        name="cast_cols",
    )(job.src)


def _ep_identity(acc):
    return acc


def _ep_silu(acc):
    return _silu(acc)


def _ep_sigmoid(acc):
    return _sigmoid(acc)


def _ep_residual(acc, x, gate):
    return x + gate * acc


def _matmul_kernel(*refs, epilogue, n_tiles, n_rows, has_cast):
    lhs_ref, rhs_ref = refs[0], refs[1]
    n_in = 2 + n_tiles + n_rows
    tile_refs = refs[2:2 + n_tiles]
    row_refs = refs[2 + n_tiles:n_in]
    o_ref = refs[n_in + has_cast]
    if has_cast:
        _run_cast(refs[n_in], refs[n_in + 2])
    acc = jnp.dot(lhs_ref[...], rhs_ref[...], preferred_element_type=jnp.float32)
    extras = [r[...] for r in tile_refs] + [r[...] for r in row_refs]
    o_ref[...] = epilogue(acc, *extras).astype(o_ref.dtype)


def _matmul(lhs, rhs, *, out_dtype, epilogue=_ep_identity, tiles=(), rows=(), tn=1024,
            cast=None, name):
    m, k = lhs.shape
    n_cols = rhs.shape[1]
    tm = 512 if tiles else 1024
    assert rhs.shape[0] == k and n_cols % tn == 0 and m % tm == 0
    grid = (m // tm, n_cols // tn)
    in_specs = [pl.BlockSpec((tm, k), lambda i, j: (i, 0)),
                pl.BlockSpec((k, tn), lambda i, j: (0, j))]
    in_specs += [pl.BlockSpec((tm, tn), lambda i, j: (i, j)) for _ in tiles]
    in_specs += [pl.BlockSpec((1, tn), lambda i, j: (0, j)) for _ in rows]
    out_bytes = jnp.dtype(out_dtype).itemsize
    tile_bytes = sum(t.dtype.itemsize for t in tiles)
    block_bytes = 2 * (tm * k * lhs.dtype.itemsize + k * tn * rhs.dtype.itemsize
                       + tm * tn * (out_bytes + tile_bytes)) + 2 * tm * tn * 4
    out_shape = jax.ShapeDtypeStruct((m, n_cols), out_dtype)
    out_spec = pl.BlockSpec((tm, tn), lambda i, j: (i, j))
    args = (lhs, rhs, *tiles, *rows)
    if cast is not None:
        c_in, c_out, c_shape, c_bytes = _cast_specs(cast, grid)
        in_specs.append(c_in)
        out_shape, out_spec = (out_shape, c_shape), (out_spec, c_out)
        args += (cast.src,)
        block_bytes += c_bytes
    return pl.pallas_call(
        functools.partial(_matmul_kernel, epilogue=epilogue, n_tiles=len(tiles),
                          n_rows=len(rows), has_cast=cast is not None),
        out_shape=out_shape,
        grid=grid,
        in_specs=in_specs,
        out_specs=out_spec,
        compiler_params=pltpu.CompilerParams(
            dimension_semantics=("arbitrary", "arbitrary"),
            vmem_limit_bytes=_vmem_limit(block_bytes)),
        name=name,
    )(*args)


def _merge_kernel(yr_ref, ya_ref, wr_ref, wa_ref, smr_ref, sma_ref, o_ref):
    p_rnn = jnp.dot(yr_ref[...], wr_ref[...], preferred_element_type=jnp.float32)
    p_attn = jnp.dot(ya_ref[...], wa_ref[...], preferred_element_type=jnp.float32)
    o_ref[...] = (smr_ref[...] * p_rnn + sma_ref[...] * p_attn).astype(o_ref.dtype)


def _merge(y_rnn, y_attn, w_rnn, w_attn, sm_rnn, sm_attn):
    m, k = y_rnn.shape
    n = w_rnn.shape[1]
    tm, tn = 512, 512
    lhs = pl.BlockSpec((tm, k), lambda i, j: (i, 0))
    rhs = pl.BlockSpec((k, tn), lambda i, j: (0, j))
    tile = pl.BlockSpec((tm, tn), lambda i, j: (i, j))
    block_bytes = 2 * (2 * tm * k * 2 + 2 * k * tn * 2 + tm * tn * (4 + 4 + 2)) + 3 * tm * tn * 4
    return pl.pallas_call(
        _merge_kernel,
        out_shape=jax.ShapeDtypeStruct((m, n), jnp.bfloat16),
        grid=(m // tm, n // tn),
        in_specs=[lhs, lhs, rhs, rhs, tile, tile],
        out_specs=tile,
        compiler_params=pltpu.CompilerParams(
            dimension_semantics=("arbitrary", "arbitrary"),
            vmem_limit_bytes=_vmem_limit(block_bytes)),
        name="merge_proj",
    )(y_rnn, y_attn, w_rnn, w_attn, sm_rnn, sm_attn)


def _out_norm_kernel(m_ref, w_ref, x_ref, gate_ref, gain_ref, o_ref, res_ref):
    j = pl.program_id(1)
    n_j = pl.num_programs(1)
    tn = w_ref.shape[1]
    acc = jnp.dot(m_ref[...], w_ref[...], preferred_element_type=jnp.float32)
    res_ref[j] = x_ref[...] + gate_ref[...] * acc

    @pl.when(j == n_j - 1)
    def _():
        n_tiles = res_ref.shape[0]
        ss = jnp.sum(res_ref[0] * res_ref[0], axis=-1, keepdims=True)
        for t in range(1, n_tiles):
            ss = ss + jnp.sum(res_ref[t] * res_ref[t], axis=-1, keepdims=True)
        inv = lax.rsqrt(ss / (n_tiles * tn) + EPS)
        for t in range(n_tiles):
            o_ref[:, t * tn:(t + 1) * tn] = res_ref[t] * inv * gain_ref[:, t * tn:(t + 1) * tn]


def _out_norm(merged, w, x2, gate, gain):
    m, k = merged.shape
    n = w.shape[1]
    tm, tn = 512, 512
    block_bytes = (2 * (tm * k * 2 + k * tn * 2 + tm * tn * 4 + tm * n * 4) + tm * n * 4
                   + 3 * tm * tn * 4)
    return pl.pallas_call(
        _out_norm_kernel,
        out_shape=jax.ShapeDtypeStruct((m, n), jnp.float32),
        grid=(m // tm, n // tn),
        in_specs=[pl.BlockSpec((tm, k), lambda i, j: (i, 0)),
                  pl.BlockSpec((k, tn), lambda i, j: (0, j)),
                  pl.BlockSpec((tm, tn), lambda i, j: (i, j)),
                  pl.BlockSpec((1, tn), lambda i, j: (0, j)),
                  pl.BlockSpec((1, n), lambda i, j: (0, 0))],
        out_specs=pl.BlockSpec((tm, n), lambda i, j: (i, 0)),
        scratch_shapes=[pltpu.VMEM((n // tn, tm, tn), jnp.float32)],
        compiler_params=pltpu.CompilerParams(
            dimension_semantics=("arbitrary", "arbitrary"),
            vmem_limit_bytes=_vmem_limit(block_bytes)),
        name="out_norm",
    )(merged, w, x2, gate, gain)


RNN_SUB_ROWS = 256
RNN_BLOCK_ROWS = 2048
RNN_HALO_GROUPS = 4
SQRT_FLOOR = 1e-30


def _slabs(ref, rows, n_slabs):
    return jnp.concatenate([ref[l, rows, :] for l in range(n_slabs)], axis=1)


def _rglru_kernel(u_ref, sg_ref, cw_ref, cb_ref, wa_ref, ba_ref, wx_ref, bx_ref, lam_ref,
                  cast_src, y_ref, cast_dst, uperm, hbuf, pbuf, carry_ref, *, conv_width):
    ts = RNN_SUB_ROWS
    tc = ts // SUBLANES
    width = u_ref.shape[1]
    n_slabs = width // LANES
    data0 = RNN_HALO_GROUPS * SUBLANES
    taps = conv_width - 1
    assert taps < RNN_HALO_GROUPS and taps <= tc
    _run_cast(cast_src, cast_dst)

    @pl.when(pl.program_id(1) == 0)
    def _():
        uperm[...] = jnp.zeros_like(uperm)
        carry_ref[...] = jnp.zeros_like(carry_ref)

    neg = -lam_ref[...]
    softplus_neg = jnp.maximum(neg, 0.0) + jnp.log1p(jnp.exp(-jnp.abs(neg)))
    half_decay = (-0.5 * LRU_C) * softplus_neg
    half_ba = 0.5 * ba_ref[...]
    half_bx = 0.5 * bx_ref[...]
    cw = cw_ref[...]
    sub = lax.broadcasted_iota(jnp.int32, (SUBLANES, width), 0)

    def group(j):
        return pl.ds(data0 + j * SUBLANES, SUBLANES)

    def sub_tile(s, _):
        r0 = pl.multiple_of(s * ts, ts)
        prev = [_slabs(uperm, group(tc - k), n_slabs) for k in range(1, taps + 1)]
        for c in range(SUBLANES):
            for l in range(n_slabs):
                uperm[l, pl.ds(data0 + c, tc, stride=SUBLANES), :] = (
                    u_ref[pl.ds(r0 + c * tc, tc), l * LANES:(l + 1) * LANES])
        for k in range(1, taps + 1):
            cur = _slabs(uperm, group(tc - k), n_slabs)
            halo = jnp.where(sub == 0, pltpu.roll(prev[k - 1], 1, axis=0),
                             pltpu.roll(cur, 1, axis=0))
            for l in range(n_slabs):
                uperm[l, group(-k), :] = halo[:, l * LANES:(l + 1) * LANES]
        u = cb_ref[...]
        for k in range(conv_width):
            first = data0 - (taps - k) * SUBLANES
            u = u + cw[k:k + 1, :] * _slabs(uperm, pl.ds(first, ts), n_slabs)
        ub = u.astype(jnp.bfloat16)
        t_r = jnp.tanh(jnp.dot(ub, wa_ref[...], preferred_element_type=jnp.float32) + half_ba)
        t_i = jnp.tanh(jnp.dot(ub, wx_ref[...], preferred_element_type=jnp.float32) + half_bx)
        log_a = half_decay * t_r + half_decay
        a = jnp.exp(log_a)
        half_u = 0.5 * u
        gated_u = half_u * t_i + half_u
        one_m_a2 = 1.0 - a * a
        b = (one_m_a2 * lax.rsqrt(jnp.maximum(one_m_a2, SQRT_FLOOR))) * gated_u

        h_loc = b[0:SUBLANES]
        p_cum = a[0:SUBLANES]
        for j in range(tc):
            if j > 0:
                aj = a[j * SUBLANES:(j + 1) * SUBLANES]
                h_loc = aj * h_loc + b[j * SUBLANES:(j + 1) * SUBLANES]
                p_cum = aj * p_cum
            pbuf[j * SUBLANES:(j + 1) * SUBLANES, :] = p_cum
            for l in range(n_slabs):
                hbuf[l, j * SUBLANES:(j + 1) * SUBLANES, :] = h_loc[:, l * LANES:(l + 1) * LANES]

        a_inc, b_inc = p_cum, h_loc
        shift = 1
        while shift < SUBLANES:
            keep = sub >= shift
            a_prev = jnp.where(keep, pltpu.roll(a_inc, shift, axis=0), 1.0)
            b_prev = jnp.where(keep, pltpu.roll(b_inc, shift, axis=0), 0.0)
            b_inc = a_inc * b_prev + b_inc
            a_inc = a_inc * a_prev
            shift *= 2
        h_in = jnp.broadcast_to(carry_ref[...], (SUBLANES, width))
        chunk_end = b_inc + a_inc * h_in
        entry = jnp.where(sub == 0, h_in, pltpu.roll(chunk_end, 1, axis=0))
        carry_ref[...] = chunk_end[SUBLANES - 1:SUBLANES, :]

        for j in range(tc):
            rows = slice(j * SUBLANES, (j + 1) * SUBLANES)
            h = _slabs(hbuf, rows, n_slabs) + pbuf[rows, :] * entry
            for l in range(n_slabs):
                hbuf[l, rows, :] = h[:, l * LANES:(l + 1) * LANES]
        for c in range(SUBLANES):
            rows = pl.ds(r0 + c * tc, tc)
            h_nat = _slabs(hbuf, pl.ds(c, tc, stride=SUBLANES), n_slabs)
            y_ref[rows, :] = (h_nat * sg_ref[rows, :]).astype(y_ref.dtype)
        return 0

    lax.fori_loop(0, u_ref.shape[0] // ts, sub_tile, 0, unroll=2)


def _rglru(u_rnn, sg_rnn, conv_w, conv_b, w_gate_a, b_gate_a, w_gate_x, b_gate_x, lam, cast):
    s, d = u_rnn.shape
    nb, bw, _ = w_gate_a.shape
    conv_width = conv_w.shape[0]
    tb = RNN_BLOCK_ROWS
    seq = pl.BlockSpec((tb, bw), lambda c, t: (t, c))
    vec = pl.BlockSpec((1, bw), lambda c, t: (0, c))
    gate_w = pl.BlockSpec((None, bw, bw), lambda c, t: (c, 0, 0))
    gate_b = pl.BlockSpec((None, 1, bw), lambda c, t: (c, 0, 0))
    block_bytes = (2 * tb * bw * (4 + 4 + 2) + 4 * bw * bw * 2
                   + 3 * (RNN_SUB_ROWS + RNN_HALO_GROUPS * SUBLANES) * bw * 4)
    grid = (nb, s // tb)
    c_in, c_out, c_shape, c_bytes = _cast_specs(cast, grid)
    return pl.pallas_call(
        functools.partial(_rglru_kernel, conv_width=conv_width),
        out_shape=(jax.ShapeDtypeStruct((s, d), jnp.bfloat16), c_shape),
        grid=grid,
        in_specs=[seq, seq,
                  pl.BlockSpec((conv_width, bw), lambda c, t: (0, c)), vec,
                  gate_w, gate_b, gate_w, gate_b, vec, c_in],
        out_specs=(seq, c_out),
        scratch_shapes=[
            pltpu.VMEM((bw // LANES, RNN_SUB_ROWS + RNN_HALO_GROUPS * SUBLANES, LANES),
                       jnp.float32),
            pltpu.VMEM((bw // LANES, RNN_SUB_ROWS, LANES), jnp.float32),
            pltpu.VMEM((RNN_SUB_ROWS, bw), jnp.float32),
            pltpu.VMEM((1, bw), jnp.float32)],
        compiler_params=pltpu.CompilerParams(
            dimension_semantics=("arbitrary", "arbitrary"),
            vmem_limit_bytes=_vmem_limit(block_bytes + c_bytes + 16 * 1024 * 1024)),
        name="rglru",
    )(u_rnn, sg_rnn, conv_w, conv_b.reshape(1, d),
      (0.5 * w_gate_a).astype(jnp.bfloat16), b_gate_a.reshape(nb, 1, bw),
      (0.5 * w_gate_x).astype(jnp.bfloat16), b_gate_x.reshape(nb, 1, bw), lam.reshape(1, d),
      cast.src)


def _swap_halves(v):
    half = v.shape[1] // 2
    return jnp.concatenate([v[:, half:], v[:, :half]], axis=1)


def _attn_block(first, sink_ref, q_ref, k_prev, k_cur, sg_ref, o_ref, *, n_kv, group, head_dim):
    bq = q_ref.shape[0]
    pair_w = 2 * head_dim
    pairs_per_group = group // 2
    d_kv = n_kv * head_dim

    k_pos = lax.broadcasted_iota(jnp.int32, (2 * bq, bq), 0)
    q_pos = lax.broadcasted_iota(jnp.int32, (2 * bq, bq), 1)
    dist = q_pos + bq - k_pos
    valid = (dist >= 0) & (dist < WINDOW) & ((k_pos >= bq) | jnp.logical_not(first))
    lane = lax.broadcasted_iota(jnp.int32, (bq, pair_w), 1)
    lo_q = lane < head_dim
    lane_kv = lax.broadcasted_iota(jnp.int32, (2 * bq, pair_w), 1)
    lo_kv = lane_kv < head_dim
    scale = head_dim ** -0.5
    zero_q = jnp.zeros((bq, pair_w), q_ref.dtype)
    ones_rows = jnp.ones((head_dim, 2 * bq), q_ref.dtype)

    for kv_pair in range(n_kv // 2):
        cols = slice(kv_pair * pair_w, (kv_pair + 1) * pair_w)
        k_two = jnp.concatenate([k_prev[:, cols], k_cur[:, cols]], axis=0)
        v_cols = slice(d_kv + kv_pair * pair_w, d_kv + (kv_pair + 1) * pair_w)
        v_two = jnp.concatenate([k_prev[:, v_cols], k_cur[:, v_cols]], axis=0)
        k_swapped = _swap_halves(k_two)
        v_t = v_two.astype(jnp.float32).T.astype(v_two.dtype)
        for parity in range(2):
            hk = 2 * kv_pair + parity
            if parity == 0:
                k_dup = jnp.where(lo_kv, k_two, k_swapped)
            else:
                k_dup = jnp.where(lo_kv, k_swapped, k_two)
            k_dup = k_dup * jnp.asarray(scale, k_dup.dtype)
            v_head_t = v_t[parity * head_dim:(parity + 1) * head_dim, :]
            q_rows = []
            for p in range(pairs_per_group):
                pair = hk * pairs_per_group + p
                qp = q_ref[:, pair * pair_w:(pair + 1) * pair_w]
                q_rows.append(jnp.where(lo_q, qp, zero_q))
                q_rows.append(jnp.where(lo_q, zero_q, qp))
            q_stack = jnp.concatenate(q_rows, axis=0)
            s_t = lax.dot_general(k_dup, q_stack, (((1,), (1,)), ((), ())),
                                  preferred_element_type=jnp.float32)
            p_cols, sink_terms = [], []
            for g in range(group):
                sink = sink_ref[hk * group + g]
                sc = jnp.where(valid, s_t[:, g * bq:(g + 1) * bq], -jnp.inf)
                m = jnp.maximum(jnp.max(sc, axis=0, keepdims=True), sink)
                p_cols.append(jnp.exp(sc - m).astype(v_two.dtype))
                sink_terms.append(jnp.exp(sink - m))
            p_t = jnp.concatenate(p_cols, axis=1)
            v_aug = jnp.concatenate([v_head_t, ones_rows], axis=0)
            o_aug = jnp.dot(v_aug, p_t, preferred_element_type=jnp.float32)
            o_t = o_aug[0:head_dim, :]
            inv_rows = [1.0 / (o_aug[head_dim:head_dim + 1, g * bq:(g + 1) * bq] + sink_terms[g])
                        for g in range(group)]
            for p in range(pairs_per_group):
                pair = hk * pairs_per_group + p
                even = o_t[:, (2 * p) * bq:(2 * p + 1) * bq] * inv_rows[2 * p]
                odd = o_t[:, (2 * p + 1) * bq:(2 * p + 2) * bq] * inv_rows[2 * p + 1]
                o_pair = jnp.concatenate([even, odd], axis=0).T
                c0 = pair * pair_w
                o_ref[:, c0:c0 + pair_w] = (o_pair * sg_ref[:, c0:c0 + pair_w]).astype(o_ref.dtype)


def _attn_kernel(sink_ref, q_ref, kv_ref, kvp_ref, sg_ref, cast_src, o_ref, cast_dst, *,
                 n_kv, group, head_dim):
    _run_cast(cast_src, cast_dst)
    bq = BLOCK_Q
    n = pl.program_id(0)
    for b in range(q_ref.shape[0] // bq):
        rows = pl.ds(b * bq, bq)
        k_prev = kvp_ref if b == 0 else kv_ref.at[pl.ds((b - 1) * bq, bq), :]
        first = (n == 0) if b == 0 else jnp.bool_(False)
        _attn_block(first, sink_ref, q_ref.at[rows, :], k_prev, kv_ref.at[rows, :],
                    sg_ref.at[rows, :], o_ref.at[rows, :], n_kv=n_kv, group=group,
                    head_dim=head_dim)


def _attention(q, kv, sg_attn, sinks, cast, *, n_kv, head_dim):
    s, d_attn = q.shape
    n_q = d_attn // head_dim
    group = n_q // n_kv
    assert head_dim * 2 == LANES and group % 2 == 0 and n_kv % 2 == 0
    bq = BLOCK_Q
    rows = ATTN_BLOCKS_PER_STEP * bq
    d_kv2 = kv.shape[1]
    grid = (s // rows, 1)
    c_in, c_out, c_shape, c_bytes = _cast_specs(cast, grid)
    block_bytes = 2 * rows * (d_attn * (2 + 4 + 2) + d_kv2 * 2) + 2 * bq * d_kv2 * 2 + c_bytes
    seq = pl.BlockSpec((rows, d_attn), lambda n, _: (n, 0))
    return pl.pallas_call(
        functools.partial(_attn_kernel, n_kv=n_kv, group=group, head_dim=head_dim),
        out_shape=(jax.ShapeDtypeStruct((s, d_attn), jnp.bfloat16), c_shape),
        grid=grid,
        in_specs=[pl.BlockSpec(memory_space=pltpu.SMEM),
                  seq,
                  pl.BlockSpec((rows, d_kv2), lambda n, _: (n, 0)),
                  pl.BlockSpec((bq, d_kv2),
                               lambda n, _: (jnp.maximum(n * ATTN_BLOCKS_PER_STEP - 1, 0), 0)),
                  seq, c_in],
        out_specs=(seq, c_out),
        compiler_params=pltpu.CompilerParams(
            dimension_semantics=("arbitrary", "arbitrary"),
            vmem_limit_bytes=_vmem_limit(block_bytes + 16 * 1024 * 1024)),
        name="swa_attn",
    )(sinks, q, kv, kv, sg_attn, cast.src)


def kernel(x, c, w_ada, b_ada, norm_gain, w_in, conv_w, conv_b, w_gate_a, b_gate_a, w_gate_x,
           b_gate_x, lru_lambda, attn_sinks, w_rnn_proj, w_attn_proj, w_out, final_norm_gain):
    bsz, seq, d = x.shape
    depth = w_in.shape[0]
    d_rnn = w_rnn_proj.shape[1]
    d_attn = w_attn_proj.shape[1]
    n_q = attn_sinks.shape[1]
    head_dim = d_attn // n_q
    d_kv = (w_in.shape[2] - 2 * d_rnn - 2 * d_attn - 2 * d) // 2
    n_kv = d_kv // head_dim
    assert bsz == 1

    xs = x.reshape(bsz * seq, d)
    bf = jnp.bfloat16
    for l in range(depth):
        mod = _modulation(c, w_ada[l], b_ada[l])
        shift, scale, gate = mod[:, :d], mod[:, d:2 * d], mod[:, 2 * d:]
        h = _norm_mod(xs, norm_gain[l].reshape(1, d), scale, shift)

        wl = w_in[l]
        c_u, c_g, c_q, c_kv = 0, d_rnn, 2 * d_rnn, 2 * d_rnn + d_attn
        c_ga = c_kv + 2 * d_kv
        c_mr, c_ma = c_ga + d_attn, c_ga + d_attn + d
        proj = functools.partial(_matmul, h)
        w_kv = _cast_cols(CastJob(wl, c_kv, 2 * d_kv))
        w_u = _cast_cols(CastJob(wl, c_u, d_rnn))
        kv = proj(w_kv, out_dtype=bf, name="proj_kv")
        u_rnn, w_g = proj(w_u, out_dtype=jnp.float32, cast=CastJob(wl, c_g, d_rnn),
                          name="proj_u")
        sg_rnn, w_q = proj(w_g, out_dtype=jnp.float32, epilogue=_ep_silu,
                           cast=CastJob(wl, c_q, d_attn), name="proj_g_rnn")
        q, w_ga = proj(w_q, out_dtype=bf, cast=CastJob(wl, c_ga, d_attn), name="proj_q")
        sg_attn, w_mr = proj(w_ga, out_dtype=jnp.float32, epilogue=_ep_silu,
                             cast=CastJob(wl, c_mr, d), name="proj_g_attn")
        sm_rnn, w_ma = proj(w_mr, out_dtype=jnp.float32, epilogue=_ep_sigmoid,
                            cast=CastJob(wl, c_ma, d), name="proj_m_rnn")
        sm_attn, w_rp = proj(w_ma, out_dtype=jnp.float32, epilogue=_ep_sigmoid,
                             cast=CastJob(w_rnn_proj[l], 0, d), name="proj_m_attn")

        y_rnn, w_ap = _rglru(u_rnn, sg_rnn, conv_w[l], conv_b[l], w_gate_a[l], b_gate_a[l],
                             w_gate_x[l], b_gate_x[l], lru_lambda[l],
                             CastJob(w_attn_proj[l], 0, d))
        y_attn, w_o = _attention(q, kv, sg_attn, attn_sinks[l], CastJob(w_out[l], 0, d),
                                 n_kv=n_kv, head_dim=head_dim)

        merged = _merge(y_rnn, y_attn, w_rp, w_ap, sm_rnn, sm_attn)
        if l + 1 < depth:
            xs = _matmul(merged, w_o, out_dtype=jnp.float32, epilogue=_ep_residual,
                         tiles=(xs,), rows=(gate,), name="out_proj")
        else:
            xs = _out_norm(merged, w_o, xs, gate, final_norm_gain.reshape(1, d))
    return xs.reshape(bsz, seq, d)
```

```python
import functools
from typing import NamedTuple

import jax
import jax.numpy as jnp
from jax import lax
from jax.experimental import pallas as pl
from jax.experimental.pallas import tpu as pltpu

EPS = 1e-6
LRU_C = 8.0
WINDOW = 128
BLOCK_Q = 128
ATTN_BLOCKS_PER_STEP = 2

V7X_VMEM_BYTES = 64 * 1024 * 1024
VMEM_HEADROOM_BYTES = 6 * 1024 * 1024
LANES = 128
SUBLANES = 8


def _vmem_limit(block_bytes):
    need = int(block_bytes) + VMEM_HEADROOM_BYTES
    assert need <= V7X_VMEM_BYTES - 2 * 1024 * 1024, need
    return need


def _sigmoid(v):
    return 0.5 * jnp.tanh(0.5 * v) + 0.5


def _silu(v):
    return v * _sigmoid(v)


def _mod_kernel(c_ref, w_ref, b_ref, o_ref):
    c_act = _silu(c_ref[...])
    lhs = jnp.broadcast_to(c_act, (SUBLANES, c_act.shape[1]))
    acc = jnp.dot(lhs, w_ref[...], preferred_element_type=jnp.float32,
                  precision=lax.Precision.HIGHEST)
    o_ref[...] = acc[0:1, :] + b_ref[...]


def _modulation(c, w_ada, b_ada):
    d, n = w_ada.shape
    tn = 1024
    return pl.pallas_call(
        _mod_kernel,
        out_shape=jax.ShapeDtypeStruct((1, n), jnp.float32),
        grid=(n // tn,),
        in_specs=[pl.BlockSpec((1, d), lambda j: (0, 0)),
                  pl.BlockSpec((d, tn), lambda j: (0, j)),
                  pl.BlockSpec((1, tn), lambda j: (0, j))],
        out_specs=pl.BlockSpec((1, tn), lambda j: (0, j)),
        compiler_params=pltpu.CompilerParams(
            dimension_semantics=("arbitrary",),
            vmem_limit_bytes=_vmem_limit(2 * d * tn * 4)),
        name="adaln_mod",
    )(c, w_ada, b_ada.reshape(1, n))


def _norm_mod_kernel(x_ref, gain_ref, scale_ref, shift_ref, o_ref):
    x = x_ref[...]
    ms = jnp.mean(x * x, axis=-1, keepdims=True)
    g = gain_ref[...] * (1.0 + scale_ref[...])
    o_ref[...] = (x * lax.rsqrt(ms + EPS) * g + shift_ref[...]).astype(o_ref.dtype)


def _norm_mod(x2, gain, scale, shift):
    s, d = x2.shape
    tm = 512
    row = pl.BlockSpec((1, d), lambda i: (0, 0))
    return pl.pallas_call(
        _norm_mod_kernel,
        out_shape=jax.ShapeDtypeStruct((s, d), jnp.bfloat16),
        grid=(s // tm,),
        in_specs=[pl.BlockSpec((tm, d), lambda i: (i, 0)), row, row, row],
        out_specs=pl.BlockSpec((tm, d), lambda i: (i, 0)),
        compiler_params=pltpu.CompilerParams(
            dimension_semantics=("arbitrary",),
            vmem_limit_bytes=_vmem_limit(2 * tm * d * (4 + 2) + 2 * tm * d * 4)),
        name="norm_mod",
    )(x2, gain, scale, shift)


CAST_BLOCK_COLS = 1024


class CastJob(NamedTuple):
    src: jax.Array
    col0: int
    cols: int


def _cast_specs(job, grid):
    g0, g1 = grid
    n_steps = g0 * g1
    rows = job.src.shape[0]
    cw = min(job.cols, CAST_BLOCK_COLS)
    n_cb = job.cols // cw
    rb = rows * n_cb // n_steps
    assert rb * n_steps == rows * n_cb and rb % 16 == 0 and job.col0 % cw == 0
    col_blk0 = job.col0 // cw

    def blk(i, j):
        s = i * g1 + j
        return s // n_cb, s % n_cb

    in_spec = pl.BlockSpec((rb, cw), lambda i, j: (blk(i, j)[0], col_blk0 + blk(i, j)[1]))
    out_spec = pl.BlockSpec((rb, cw), lambda i, j: blk(i, j))
    out_shape = jax.ShapeDtypeStruct((rows, job.cols), jnp.bfloat16)
    return in_spec, out_spec, out_shape, 2 * rb * cw * (4 + 2)


def _run_cast(src_ref, dst_ref):
    dst_ref[...] = src_ref[...].astype(dst_ref.dtype)


def _cast_cols(job):
    rows = job.src.shape[0]
    rb = 512
    cw = min(job.cols, CAST_BLOCK_COLS)
    assert rows % rb == 0 and job.cols % cw == 0 and job.col0 % cw == 0
    col_blk0 = job.col0 // cw
    return pl.pallas_call(
        _run_cast,
        out_shape=jax.ShapeDtypeStruct((rows, job.cols), jnp.bfloat16),
        grid=(rows // rb, job.cols // cw),
        in_specs=[pl.BlockSpec((rb, cw), lambda i, j: (i, col_blk0 + j))],
        out_specs=pl.BlockSpec((rb, cw), lambda i, j: (i, j)),
        compiler_params=pltpu.CompilerParams(
            dimension_semantics=("arbitrary", "arbitrary"),
            vmem_limit_bytes=_vmem_limit(2 * rb * cw * (4 + 2))),
        name="cast_cols",
    )(job.src)


def _ep_identity(acc):
    return acc


def _ep_silu(acc):
    return _silu(acc)


def _ep_sigmoid(acc):
    return _sigmoid(acc)


def _ep_residual(acc, x, gate):
    return x + gate * acc


def _matmul_kernel(*refs, epilogue, n_tiles, n_rows, has_cast):
    lhs_ref, rhs_ref = refs[0], refs[1]
    n_in = 2 + n_tiles + n_rows
    tile_refs = refs[2:2 + n_tiles]
    row_refs = refs[2 + n_tiles:n_in]
    o_ref = refs[n_in + has_cast]
    if has_cast:
        _run_cast(refs[n_in], refs[n_in + 2])
    acc = jnp.dot(lhs_ref[...], rhs_ref[...], preferred_element_type=jnp.float32)
    extras = [r[...] for r in tile_refs] + [r[...] for r in row_refs]
    o_ref[...] = epilogue(acc, *extras).astype(o_ref.dtype)


def _matmul(lhs, rhs, *, out_dtype, epilogue=_ep_identity, tiles=(), rows=(), tn=1024,
            cast=None, name):
    m, k = lhs.shape
    n_cols = rhs.shape[1]
    tm = 512 if tiles else 1024
    assert rhs.shape[0] == k and n_cols % tn == 0 and m % tm == 0
    grid = (m // tm, n_cols // tn)
    in_specs = [pl.BlockSpec((tm, k), lambda i, j: (i, 0)),
                pl.BlockSpec((k, tn), lambda i, j: (0, j))]
    in_specs += [pl.BlockSpec((tm, tn), lambda i, j: (i, j)) for _ in tiles]
    in_specs += [pl.BlockSpec((1, tn), lambda i, j: (0, j)) for _ in rows]
    out_bytes = jnp.dtype(out_dtype).itemsize
    tile_bytes = sum(t.dtype.itemsize for t in tiles)
    block_bytes = 2 * (tm * k * lhs.dtype.itemsize + k * tn * rhs.dtype.itemsize
                       + tm * tn * (out_bytes + tile_bytes)) + 2 * tm * tn * 4
    out_shape = jax.ShapeDtypeStruct((m, n_cols), out_dtype)
    out_spec = pl.BlockSpec((tm, tn), lambda i, j: (i, j))
    args = (lhs, rhs, *tiles, *rows)
    if cast is not None:
        c_in, c_out, c_shape, c_bytes = _cast_specs(cast, grid)
        in_specs.append(c_in)
        out_shape, out_spec = (out_shape, c_shape), (out_spec, c_out)
        args += (cast.src,)
        block_bytes += c_bytes
    return pl.pallas_call(
        functools.partial(_matmul_kernel, epilogue=epilogue, n_tiles=len(tiles),
                          n_rows=len(rows), has_cast=cast is not None),
        out_shape=out_shape,
        grid=grid,
        in_specs=in_specs,
        out_specs=out_spec,
        compiler_params=pltpu.CompilerParams(
            dimension_semantics=("arbitrary", "arbitrary"),
            vmem_limit_bytes=_vmem_limit(block_bytes)),
        name=name,
    )(*args)


def _merge_kernel(yr_ref, ya_ref, wr_ref, wa_ref, smr_ref, sma_ref, o_ref):
    p_rnn = jnp.dot(yr_ref[...], wr_ref[...], preferred_element_type=jnp.float32)
    p_attn = jnp.dot(ya_ref[...], wa_ref[...], preferred_element_type=jnp.float32)
    o_ref[...] = (smr_ref[...] * p_rnn + sma_ref[...] * p_attn).astype(o_ref.dtype)


def _merge(y_rnn, y_attn, w_rnn, w_attn, sm_rnn, sm_attn):
    m, k = y_rnn.shape
    n = w_rnn.shape[1]
    tm, tn = 512, 512
    lhs = pl.BlockSpec((tm, k), lambda i, j: (i, 0))
    rhs = pl.BlockSpec((k, tn), lambda i, j: (0, j))
    tile = pl.BlockSpec((tm, tn), lambda i, j: (i, j))
    block_bytes = 2 * (2 * tm * k * 2 + 2 * k * tn * 2 + tm * tn * (4 + 4 + 2)) + 3 * tm * tn * 4
    return pl.pallas_call(
        _merge_kernel,
        out_shape=jax.ShapeDtypeStruct((m, n), jnp.bfloat16),
        grid=(m // tm, n // tn),
        in_specs=[lhs, lhs, rhs, rhs, tile, tile],
        out_specs=tile,
        compiler_params=pltpu.CompilerParams(
            dimension_semantics=("arbitrary", "arbitrary"),
            vmem_limit_bytes=_vmem_limit(block_bytes)),
        name="merge_proj",
    )(y_rnn, y_attn, w_rnn, w_attn, sm_rnn, sm_attn)


def _out_norm_kernel(m_ref, w_ref, x_ref, gate_ref, gain_ref, o_ref, res_ref):
    j = pl.program_id(1)
    n_j = pl.num_programs(1)
    tn = w_ref.shape[1]
    acc = jnp.dot(m_ref[...], w_ref[...], preferred_element_type=jnp.float32)
    res_ref[j] = x_ref[...] + gate_ref[...] * acc

    @pl.when(j == n_j - 1)
    def _():
        n_tiles = res_ref.shape[0]
        ss = jnp.sum(res_ref[0] * res_ref[0], axis=-1, keepdims=True)
        for t in range(1, n_tiles):
            ss = ss + jnp.sum(res_ref[t] * res_ref[t], axis=-1, keepdims=True)
        inv = lax.rsqrt(ss / (n_tiles * tn) + EPS)
        for t in range(n_tiles):
            o_ref[:, t * tn:(t + 1) * tn] = res_ref[t] * inv * gain_ref[:, t * tn:(t + 1) * tn]


def _out_norm(merged, w, x2, gate, gain):
    m, k = merged.shape
    n = w.shape[1]
    tm, tn = 512, 1024
    block_bytes = (2 * (tm * k * 2 + k * tn * 2 + tm * tn * 4 + tm * n * 4) + tm * n * 4
                   + tm * tn * 4)
    return pl.pallas_call(
        _out_norm_kernel,
        out_shape=jax.ShapeDtypeStruct((m, n), jnp.float32),
        grid=(m // tm, n // tn),
        in_specs=[pl.BlockSpec((tm, k), lambda i, j: (i, 0)),
                  pl.BlockSpec((k, tn), lambda i, j: (0, j)),
                  pl.BlockSpec((tm, tn), lambda i, j: (i, j)),
                  pl.BlockSpec((1, tn), lambda i, j: (0, j)),
                  pl.BlockSpec((1, n), lambda i, j: (0, 0))],
        out_specs=pl.BlockSpec((tm, n), lambda i, j: (i, 0)),
        scratch_shapes=[pltpu.VMEM((n // tn, tm, tn), jnp.float32)],
        compiler_params=pltpu.CompilerParams(
            dimension_semantics=("arbitrary", "arbitrary"),
            vmem_limit_bytes=_vmem_limit(block_bytes)),
        name="out_norm",
    )(merged, w, x2, gate, gain)


RNN_SUB_ROWS = 512
RNN_BLOCK_ROWS = 2048
RNN_HALO_GROUPS = 4
SQRT_FLOOR = 1e-30


def _slabs(ref, rows, n_slabs):
    return jnp.concatenate([ref[l, rows, :] for l in range(n_slabs)], axis=1)


def _rglru_kernel(u_ref, sg_ref, cw_ref, cb_ref, wa_ref, ba_ref, wx_ref, bx_ref, lam_ref,
                  cast_src, y_ref, cast_dst, uperm, hbuf, pbuf, carry_ref, *, conv_width):
    ts = RNN_SUB_ROWS
    tc = ts // SUBLANES
    width = u_ref.shape[1]
    n_slabs = width // LANES
    data0 = RNN_HALO_GROUPS * SUBLANES
    taps = conv_width - 1
    assert taps < RNN_HALO_GROUPS and taps <= tc
    _run_cast(cast_src, cast_dst)

    @pl.when(pl.program_id(1) == 0)
    def _():
        uperm[...] = jnp.zeros_like(uperm)
        carry_ref[...] = jnp.zeros_like(carry_ref)

    neg = -lam_ref[...]
    softplus_neg = jnp.maximum(neg, 0.0) + jnp.log1p(jnp.exp(-jnp.abs(neg)))
    half_decay = (-0.5 * LRU_C) * softplus_neg
    half_ba = 0.5 * ba_ref[...]
    half_bx = 0.5 * bx_ref[...]
    cw = cw_ref[...]
    sub = lax.broadcasted_iota(jnp.int32, (SUBLANES, width), 0)

    def group(j):
        return pl.ds(data0 + j * SUBLANES, SUBLANES)

    def sub_tile(s, _):
        r0 = pl.multiple_of(s * ts, ts)
        prev = [_slabs(uperm, group(tc - k), n_slabs) for k in range(1, taps + 1)]
        for c in range(SUBLANES):
            for l in range(n_slabs):
                uperm[l, pl.ds(data0 + c, tc, stride=SUBLANES), :] = (
                    u_ref[pl.ds(r0 + c * tc, tc), l * LANES:(l + 1) * LANES])
        for k in range(1, taps + 1):
            cur = _slabs(uperm, group(tc - k), n_slabs)
            halo = jnp.where(sub == 0, pltpu.roll(prev[k - 1], 1, axis=0),
                             pltpu.roll(cur, 1, axis=0))
            for l in range(n_slabs):
                uperm[l, group(-k), :] = halo[:, l * LANES:(l + 1) * LANES]
        u = cb_ref[...]
        for k in range(conv_width):
            first = data0 - (taps - k) * SUBLANES
            u = u + cw[k:k + 1, :] * _slabs(uperm, pl.ds(first, ts), n_slabs)
        ub = u.astype(jnp.bfloat16)
        t_r = jnp.tanh(jnp.dot(ub, wa_ref[...], preferred_element_type=jnp.float32) + half_ba)
        t_i = jnp.tanh(jnp.dot(ub, wx_ref[...], preferred_element_type=jnp.float32) + half_bx)
        log_a = half_decay * t_r + half_decay
        a = jnp.exp(log_a)
        half_u = 0.5 * u
        gated_u = half_u * t_i + half_u
        one_m_a2 = 1.0 - a * a
        b = (one_m_a2 * lax.rsqrt(jnp.maximum(one_m_a2, SQRT_FLOOR))) * gated_u

        h_loc = b[0:SUBLANES]
        p_cum = a[0:SUBLANES]
        for j in range(tc):
            if j > 0:
                aj = a[j * SUBLANES:(j + 1) * SUBLANES]
                h_loc = aj * h_loc + b[j * SUBLANES:(j + 1) * SUBLANES]
                p_cum = aj * p_cum
            pbuf[j * SUBLANES:(j + 1) * SUBLANES, :] = p_cum
            for l in range(n_slabs):
                hbuf[l, j * SUBLANES:(j + 1) * SUBLANES, :] = h_loc[:, l * LANES:(l + 1) * LANES]

        a_inc, b_inc = p_cum, h_loc
        shift = 1
        while shift < SUBLANES:
            keep = sub >= shift
            a_prev = jnp.where(keep, pltpu.roll(a_inc, shift, axis=0), 1.0)
            b_prev = jnp.where(keep, pltpu.roll(b_inc, shift, axis=0), 0.0)
            b_inc = a_inc * b_prev + b_inc
            a_inc = a_inc * a_prev
            shift *= 2
        h_in = jnp.broadcast_to(carry_ref[...], (SUBLANES, width))
        chunk_end = b_inc + a_inc * h_in
        entry = jnp.where(sub == 0, h_in, pltpu.roll(chunk_end, 1, axis=0))
        carry_ref[...] = chunk_end[SUBLANES - 1:SUBLANES, :]

        for j in range(tc):
            rows = slice(j * SUBLANES, (j + 1) * SUBLANES)
            h = _slabs(hbuf, rows, n_slabs) + pbuf[rows, :] * entry
            for l in range(n_slabs):
                hbuf[l, rows, :] = h[:, l * LANES:(l + 1) * LANES]
        for c in range(SUBLANES):
            rows = pl.ds(r0 + c * tc, tc)
            h_nat = _slabs(hbuf, pl.ds(c, tc, stride=SUBLANES), n_slabs)
            y_ref[rows, :] = (h_nat * sg_ref[rows, :]).astype(y_ref.dtype)
        return 0

    lax.fori_loop(0, u_ref.shape[0] // ts, sub_tile, 0, unroll=2)


def _rglru(u_rnn, sg_rnn, conv_w, conv_b, w_gate_a, b_gate_a, w_gate_x, b_gate_x, lam, cast):
    s, d = u_rnn.shape
    nb, bw, _ = w_gate_a.shape
    conv_width = conv_w.shape[0]
    tb = RNN_BLOCK_ROWS
    seq = pl.BlockSpec((tb, bw), lambda c, t: (t, c))
    vec = pl.BlockSpec((1, bw), lambda c, t: (0, c))
    gate_w = pl.BlockSpec((None, bw, bw), lambda c, t: (c, 0, 0))
    gate_b = pl.BlockSpec((None, 1, bw), lambda c, t: (c, 0, 0))
    block_bytes = (2 * tb * bw * (4 + 4 + 2) + 4 * bw * bw * 2
                   + 3 * (RNN_SUB_ROWS + RNN_HALO_GROUPS * SUBLANES) * bw * 4)
    grid = (nb, s // tb)
    c_in, c_out, c_shape, c_bytes = _cast_specs(cast, grid)
    return pl.pallas_call(
        functools.partial(_rglru_kernel, conv_width=conv_width),
        out_shape=(jax.ShapeDtypeStruct((s, d), jnp.bfloat16), c_shape),
        grid=grid,
        in_specs=[seq, seq,
                  pl.BlockSpec((conv_width, bw), lambda c, t: (0, c)), vec,
                  gate_w, gate_b, gate_w, gate_b, vec, c_in],
        out_specs=(seq, c_out),
        scratch_shapes=[
            pltpu.VMEM((bw // LANES, RNN_SUB_ROWS + RNN_HALO_GROUPS * SUBLANES, LANES),
                       jnp.float32),
            pltpu.VMEM((bw // LANES, RNN_SUB_ROWS, LANES), jnp.float32),
            pltpu.VMEM((RNN_SUB_ROWS, bw), jnp.float32),
            pltpu.VMEM((1, bw), jnp.float32)],
        compiler_params=pltpu.CompilerParams(
            dimension_semantics=("arbitrary", "arbitrary"),
            vmem_limit_bytes=_vmem_limit(block_bytes + c_bytes + 16 * 1024 * 1024)),
        name="rglru",
    )(u_rnn, sg_rnn, conv_w, conv_b.reshape(1, d),
      (0.5 * w_gate_a).astype(jnp.bfloat16), b_gate_a.reshape(nb, 1, bw),
      (0.5 * w_gate_x).astype(jnp.bfloat16), b_gate_x.reshape(nb, 1, bw), lam.reshape(1, d),
      cast.src)


def _swap_halves(v):
    half = v.shape[1] // 2
    return jnp.concatenate([v[:, half:], v[:, :half]], axis=1)


def _attn_block(first, sink_ref, q_ref, k_prev, k_cur, sg_ref, o_ref, *, n_kv, group, head_dim):
    bq = q_ref.shape[0]
    pair_w = 2 * head_dim
    pairs_per_group = group // 2
    d_kv = n_kv * head_dim

    q_pos = lax.broadcasted_iota(jnp.int32, (bq, 2 * bq), 0)
    k_pos = lax.broadcasted_iota(jnp.int32, (bq, 2 * bq), 1)
    dist = q_pos + bq - k_pos
    valid = (dist >= 0) & (dist < WINDOW) & ((k_pos >= bq) | jnp.logical_not(first))
    lane = lax.broadcasted_iota(jnp.int32, (bq, pair_w), 1)
    lo_q = lane < head_dim
    lane_kv = lax.broadcasted_iota(jnp.int32, (2 * bq, pair_w), 1)
    lo_kv = lane_kv < head_dim
    scale = head_dim ** -0.5
    zero_q = jnp.zeros((bq, pair_w), q_ref.dtype)

    for kv_pair in range(n_kv // 2):
        cols = slice(kv_pair * pair_w, (kv_pair + 1) * pair_w)
        k_two = jnp.concatenate([k_prev[:, cols], k_cur[:, cols]], axis=0)
        v_cols = slice(d_kv + kv_pair * pair_w, d_kv + (kv_pair + 1) * pair_w)
        v_two = jnp.concatenate([k_prev[:, v_cols], k_cur[:, v_cols]], axis=0)
        k_swapped = _swap_halves(k_two)
        v_swapped = _swap_halves(v_two)
        for parity in range(2):
            hk = 2 * kv_pair + parity
            if parity == 0:
                k_dup = jnp.where(lo_kv, k_two, k_swapped)
                v_dup = jnp.where(lo_kv, v_two, v_swapped)
            else:
                k_dup = jnp.where(lo_kv, k_swapped, k_two)
                v_dup = jnp.where(lo_kv, v_swapped, v_two)
            k_dup = k_dup * jnp.asarray(scale, k_dup.dtype)
            q_rows = []
            for p in range(pairs_per_group):
                pair = hk * pairs_per_group + p
                qp = q_ref[:, pair * pair_w:(pair + 1) * pair_w]
                q_rows.append(jnp.where(lo_q, qp, zero_q))
                q_rows.append(jnp.where(lo_q, zero_q, qp))
            q_stack = jnp.concatenate(q_rows, axis=0)
            s_all = lax.dot_general(q_stack, k_dup, (((1,), (1,)), ((), ())),
                                    preferred_element_type=jnp.float32)
            p_rows, inv_rows = [], []
            for g in range(group):
                sink = sink_ref[hk * group + g]
                sc = jnp.where(valid, s_all[g * bq:(g + 1) * bq, :], -jnp.inf)
                m = jnp.maximum(jnp.max(sc, axis=-1, keepdims=True), sink)
                e = jnp.exp(sc - m)
                denom = jnp.sum(e, axis=-1, keepdims=True) + jnp.exp(sink - m)
                p_rows.append(e.astype(v_dup.dtype))
                inv_rows.append(1.0 / denom)
            p_stack = jnp.concatenate(p_rows, axis=0)
            o_all = jnp.dot(p_stack, v_dup, preferred_element_type=jnp.float32)
            for p in range(pairs_per_group):
                pair = hk * pairs_per_group + p
                o_even = o_all[(2 * p) * bq:(2 * p + 1) * bq, :] * inv_rows[2 * p]
                o_odd = o_all[(2 * p + 1) * bq:(2 * p + 2) * bq, :] * inv_rows[2 * p + 1]
                o_pair = jnp.where(lo_q, o_even, o_odd)
                c0 = pair * pair_w
                o_ref[:, c0:c0 + pair_w] = (o_pair * sg_ref[:, c0:c0 + pair_w]).astype(o_ref.dtype)


def _attn_kernel(sink_ref, q_ref, kv_ref, kvp_ref, sg_ref, cast_src, o_ref, cast_dst, *,
                 n_kv, group, head_dim):
    _run_cast(cast_src, cast_dst)
    bq = BLOCK_Q
    n = pl.program_id(0)
    for b in range(q_ref.shape[0] // bq):
        rows = pl.ds(b * bq, bq)
        k_prev = kvp_ref if b == 0 else kv_ref.at[pl.ds((b - 1) * bq, bq), :]
        first = (n == 0) if b == 0 else jnp.bool_(False)
        _attn_block(first, sink_ref, q_ref.at[rows, :], k_prev, kv_ref.at[rows, :],
                    sg_ref.at[rows, :], o_ref.at[rows, :], n_kv=n_kv, group=group,
                    head_dim=head_dim)


def _attention(q, kv, sg_attn, sinks, cast, *, n_kv, head_dim):
    s, d_attn = q.shape
    n_q = d_attn // head_dim
    group = n_q // n_kv
    assert head_dim * 2 == LANES and group % 2 == 0 and n_kv % 2 == 0
    bq = BLOCK_Q
    rows = ATTN_BLOCKS_PER_STEP * bq
    d_kv2 = kv.shape[1]
    grid = (s // rows, 1)
    c_in, c_out, c_shape, c_bytes = _cast_specs(cast, grid)
    block_bytes = 2 * rows * (d_attn * (2 + 4 + 2) + d_kv2 * 2) + 2 * bq * d_kv2 * 2 + c_bytes
    seq = pl.BlockSpec((rows, d_attn), lambda n, _: (n, 0))
    return pl.pallas_call(
        functools.partial(_attn_kernel, n_kv=n_kv, group=group, head_dim=head_dim),
        out_shape=(jax.ShapeDtypeStruct((s, d_attn), jnp.bfloat16), c_shape),
        grid=grid,
        in_specs=[pl.BlockSpec(memory_space=pltpu.SMEM),
                  seq,
                  pl.BlockSpec((rows, d_kv2), lambda n, _: (n, 0)),
                  pl.BlockSpec((bq, d_kv2),
                               lambda n, _: (jnp.maximum(n * ATTN_BLOCKS_PER_STEP - 1, 0), 0)),
                  seq, c_in],
        out_specs=(seq, c_out),
        compiler_params=pltpu.CompilerParams(
            dimension_semantics=("arbitrary", "arbitrary"),
            vmem_limit_bytes=_vmem_limit(block_bytes + 16 * 1024 * 1024)),
        name="swa_attn",
    )(sinks, q, kv, kv, sg_attn, cast.src)


def kernel(x, c, w_ada, b_ada, norm_gain, w_in, conv_w, conv_b, w_gate_a, b_gate_a, w_gate_x,
           b_gate_x, lru_lambda, attn_sinks, w_rnn_proj, w_attn_proj, w_out, final_norm_gain):
    bsz, seq, d = x.shape
    depth = w_in.shape[0]
    d_rnn = w_rnn_proj.shape[1]
    d_attn = w_attn_proj.shape[1]
    n_q = attn_sinks.shape[1]
    head_dim = d_attn // n_q
    d_kv = (w_in.shape[2] - 2 * d_rnn - 2 * d_attn - 2 * d) // 2
    n_kv = d_kv // head_dim
    assert bsz == 1

    xs = x.reshape(bsz * seq, d)
    bf = jnp.bfloat16
    for l in range(depth):
        mod = _modulation(c, w_ada[l], b_ada[l])
        shift, scale, gate = mod[:, :d], mod[:, d:2 * d], mod[:, 2 * d:]
        h = _norm_mod(xs, norm_gain[l].reshape(1, d), scale, shift)

        wl = w_in[l]
        c_u, c_g, c_q, c_kv = 0, d_rnn, 2 * d_rnn, 2 * d_rnn + d_attn
        c_ga = c_kv + 2 * d_kv
        c_mr, c_ma = c_ga + d_attn, c_ga + d_attn + d
        proj = functools.partial(_matmul, h)
        w_kv = _cast_cols(CastJob(wl, c_kv, 2 * d_kv))
        w_u = _cast_cols(CastJob(wl, c_u, d_rnn))
        kv = proj(w_kv, out_dtype=bf, name="proj_kv")
        u_rnn, w_g = proj(w_u, out_dtype=jnp.float32, cast=CastJob(wl, c_g, d_rnn),
                          name="proj_u")
        sg_rnn, w_q = proj(w_g, out_dtype=jnp.float32, epilogue=_ep_silu,
                           cast=CastJob(wl, c_q, d_attn), name="proj_g_rnn")
        q, w_ga = proj(w_q, out_dtype=bf, cast=CastJob(wl, c_ga, d_attn), name="proj_q")
        sg_attn, w_mr = proj(w_ga, out_dtype=jnp.float32, epilogue=_ep_silu,
                             cast=CastJob(wl, c_mr, d), name="proj_g_attn")
        sm_rnn, w_ma = proj(w_mr, out_dtype=jnp.float32, epilogue=_ep_sigmoid,
                            cast=CastJob(wl, c_ma, d), name="proj_m_rnn")
        sm_attn, w_rp = proj(w_ma, out_dtype=jnp.float32, epilogue=_ep_sigmoid,
                             cast=CastJob(w_rnn_proj[l], 0, d), name="proj_m_attn")

        y_rnn, w_ap = _rglru(u_rnn, sg_rnn, conv_w[l], conv_b[l], w_gate_a[l], b_gate_a[l],
                             w_gate_x[l], b_gate_x[l], lru_lambda[l],
                             CastJob(w_attn_proj[l], 0, d))
        y_attn, w_o = _attention(q, kv, sg_attn, attn_sinks[l], CastJob(w_out[l], 0, d),
                                 n_kv=n_kv, head_dim=head_dim)

        merged = _merge(y_rnn, y_attn, w_rp, w_ap, sm_rnn, sm_attn)
        if l + 1 < depth:
            xs = _matmul(merged, w_o, out_dtype=jnp.float32, epilogue=_ep_residual,
                         tiles=(xs,), rows=(gate,), name="out_proj")
        else:
            xs = _out_norm(merged, w_o, xs, gate, final_norm_gain.reshape(1, d))
    return xs.reshape(bsz, seq, d)
```

```python
import functools
from typing import NamedTuple

import jax
import jax.numpy as jnp
from jax import lax
from jax.experimental import pallas as pl
from jax.experimental.pallas import tpu as pltpu

EPS = 1e-6
LRU_C = 8.0
WINDOW = 128
BLOCK_Q = 128
ATTN_BLOCKS_PER_STEP = 2

V7X_VMEM_BYTES = 64 * 1024 * 1024
VMEM_HEADROOM_BYTES = 6 * 1024 * 1024
LANES = 128
SUBLANES = 8


def _vmem_limit(block_bytes):
    need = int(block_bytes) + VMEM_HEADROOM_BYTES
    assert need <= V7X_VMEM_BYTES - 2 * 1024 * 1024, need
    return need


def _sigmoid(v):
    return 0.5 * jnp.tanh(0.5 * v) + 0.5


def _silu(v):
    return v * _sigmoid(v)


def _mod_kernel(c_ref, w_ref, b_ref, o_ref):
    c_act = _silu(c_ref[...])
    o_ref[...] = jnp.sum(w_ref[...] * c_act, axis=0, keepdims=True) + b_ref[...]


def _modulation(c, w_ada, b_ada):
    d, n = w_ada.shape
    tn = 512
    return pl.pallas_call(
        _mod_kernel,
        out_shape=jax.ShapeDtypeStruct((1, n), jnp.float32),
        grid=(n // tn,),
        in_specs=[pl.BlockSpec((d, 1), lambda j: (0, 0)),
                  pl.BlockSpec((d, tn), lambda j: (0, j)),
                  pl.BlockSpec((1, tn), lambda j: (0, j))],
        out_specs=pl.BlockSpec((1, tn), lambda j: (0, j)),
        compiler_params=pltpu.CompilerParams(
            dimension_semantics=("arbitrary",),
            vmem_limit_bytes=_vmem_limit(2 * d * tn * 4 + 2 * d * LANES * 4 + d * tn * 4)),
        name="adaln_mod",
    )(c.reshape(d, 1), w_ada, b_ada.reshape(1, n))


def _norm_mod_kernel(x_ref, gain_ref, scale_ref, shift_ref, o_ref):
    x = x_ref[...]
    ms = jnp.mean(x * x, axis=-1, keepdims=True)
    g = gain_ref[...] * (1.0 + scale_ref[...])
    o_ref[...] = (x * lax.rsqrt(ms + EPS) * g + shift_ref[...]).astype(o_ref.dtype)


def _norm_mod(x2, gain, scale, shift):
    s, d = x2.shape
    tm = 512
    row = pl.BlockSpec((1, d), lambda i: (0, 0))
    return pl.pallas_call(
        _norm_mod_kernel,
        out_shape=jax.ShapeDtypeStruct((s, d), jnp.bfloat16),
        grid=(s // tm,),
        in_specs=[pl.BlockSpec((tm, d), lambda i: (i, 0)), row, row, row],
        out_specs=pl.BlockSpec((tm, d), lambda i: (i, 0)),
        compiler_params=pltpu.CompilerParams(
            dimension_semantics=("arbitrary",),
            vmem_limit_bytes=_vmem_limit(2 * tm * d * (4 + 2) + 2 * tm * d * 4)),
        name="norm_mod",
    )(x2, gain, scale, shift)


CAST_BLOCK_COLS = 1024


class CastJob(NamedTuple):
    src: jax.Array
    col0: int
    cols: int


def _cast_specs(job, grid):
    g0, g1 = grid
    n_steps = g0 * g1
    rows = job.src.shape[0]
    cw = min(job.cols, CAST_BLOCK_COLS)
    n_cb = job.cols // cw
    rb = rows * n_cb // n_steps
    assert rb * n_steps == rows * n_cb and rb % 16 == 0 and job.col0 % cw == 0
    col_blk0 = job.col0 // cw

    def blk(i, j):
        s = i * g1 + j
        return s // n_cb, s % n_cb

    in_spec = pl.BlockSpec((rb, cw), lambda i, j: (blk(i, j)[0], col_blk0 + blk(i, j)[1]))
    out_spec = pl.BlockSpec((rb, cw), lambda i, j: blk(i, j))
    out_shape = jax.ShapeDtypeStruct((rows, job.cols), jnp.bfloat16)
    return in_spec, out_spec, out_shape, 2 * rb * cw * (4 + 2)


def _run_cast(src_ref, dst_ref):
    dst_ref[...] = src_ref[...].astype(dst_ref.dtype)


def _cast_cols(job):
    rows = job.src.shape[0]
    rb = 512
    cw = min(job.cols, CAST_BLOCK_COLS)
    assert rows % rb == 0 and job.cols % cw == 0 and job.col0 % cw == 0
    col_blk0 = job.col0 // cw
    return pl.pallas_call(
        _run_cast,
        out_shape=jax.ShapeDtypeStruct((rows, job.cols), jnp.bfloat16),
        grid=(rows // rb, job.cols // cw),
        in_specs=[pl.BlockSpec((rb, cw), lambda i, j: (i, col_blk0 + j))],
        out_specs=pl.BlockSpec((rb, cw), lambda i, j: (i, j)),
        compiler_params=pltpu.CompilerParams(
            dimension_semantics=("arbitrary", "arbitrary"),
            vmem_limit_bytes=_vmem_limit(2 * rb * cw * (4 + 2))),
        name="cast_cols",
    )(job.src)


def _ep_identity(acc):
    return acc


def _ep_silu(acc):
    return _silu(acc)


def _ep_sigmoid(acc):
    return _sigmoid(acc)


def _ep_residual(acc, x, gate):
    return x + gate * acc


def _matmul_kernel(*refs, epilogue, n_tiles, n_rows, has_cast):
    lhs_ref, rhs_ref = refs[0], refs[1]
    n_in = 2 + n_tiles + n_rows
    tile_refs = refs[2:2 + n_tiles]
    row_refs = refs[2 + n_tiles:n_in]
    o_ref = refs[n_in + has_cast]
    if has_cast:
        _run_cast(refs[n_in], refs[n_in + 2])
    acc = jnp.dot(lhs_ref[...], rhs_ref[...], preferred_element_type=jnp.float32)
    extras = [r[...] for r in tile_refs] + [r[...] for r in row_refs]
    o_ref[...] = epilogue(acc, *extras).astype(o_ref.dtype)


def _matmul(lhs, rhs, *, out_dtype, epilogue=_ep_identity, tiles=(), rows=(), tn=1024,
            cast=None, name):
    m, k = lhs.shape
    n_cols = rhs.shape[1]
    tm = 512 if tiles else 1024
    assert rhs.shape[0] == k and n_cols % tn == 0 and m % tm == 0
    grid = (m // tm, n_cols // tn)
    in_specs = [pl.BlockSpec((tm, k), lambda i, j: (i, 0)),
                pl.BlockSpec((k, tn), lambda i, j: (0, j))]
    in_specs += [pl.BlockSpec((tm, tn), lambda i, j: (i, j)) for _ in tiles]
    in_specs += [pl.BlockSpec((1, tn), lambda i, j: (0, j)) for _ in rows]
    out_bytes = jnp.dtype(out_dtype).itemsize
    tile_bytes = sum(t.dtype.itemsize for t in tiles)
    block_bytes = 2 * (tm * k * lhs.dtype.itemsize + k * tn * rhs.dtype.itemsize
                       + tm * tn * (out_bytes + tile_bytes)) + 2 * tm * tn * 4
    out_shape = jax.ShapeDtypeStruct((m, n_cols), out_dtype)
    out_spec = pl.BlockSpec((tm, tn), lambda i, j: (i, j))
    args = (lhs, rhs, *tiles, *rows)
    if cast is not None:
        c_in, c_out, c_shape, c_bytes = _cast_specs(cast, grid)
        in_specs.append(c_in)
        out_shape, out_spec = (out_shape, c_shape), (out_spec, c_out)
        args += (cast.src,)
        block_bytes += c_bytes
    return pl.pallas_call(
        functools.partial(_matmul_kernel, epilogue=epilogue, n_tiles=len(tiles),
                          n_rows=len(rows), has_cast=cast is not None),
        out_shape=out_shape,
        grid=grid,
        in_specs=in_specs,
        out_specs=out_spec,
        compiler_params=pltpu.CompilerParams(
            dimension_semantics=("arbitrary", "arbitrary"),
            vmem_limit_bytes=_vmem_limit(block_bytes)),
        name=name,
    )(*args)


def _merge_kernel(yr_ref, ya_ref, wr_ref, wa_ref, smr_ref, sma_ref, o_ref):
    p_rnn = jnp.dot(yr_ref[...], wr_ref[...], preferred_element_type=jnp.float32)
    p_attn = jnp.dot(ya_ref[...], wa_ref[...], preferred_element_type=jnp.float32)
    o_ref[...] = (smr_ref[...] * p_rnn + sma_ref[...] * p_attn).astype(o_ref.dtype)


def _merge(y_rnn, y_attn, w_rnn, w_attn, sm_rnn, sm_attn):
    m, k = y_rnn.shape
    n = w_rnn.shape[1]
    tm, tn = 512, 512
    lhs = pl.BlockSpec((tm, k), lambda i, j: (i, 0))
    rhs = pl.BlockSpec((k, tn), lambda i, j: (0, j))
    tile = pl.BlockSpec((tm, tn), lambda i, j: (i, j))
    block_bytes = 2 * (2 * tm * k * 2 + 2 * k * tn * 2 + tm * tn * (4 + 4 + 2)) + 3 * tm * tn * 4
    return pl.pallas_call(
        _merge_kernel,
        out_shape=jax.ShapeDtypeStruct((m, n), jnp.bfloat16),
        grid=(m // tm, n // tn),
        in_specs=[lhs, lhs, rhs, rhs, tile, tile],
        out_specs=tile,
        compiler_params=pltpu.CompilerParams(
            dimension_semantics=("arbitrary", "arbitrary"),
            vmem_limit_bytes=_vmem_limit(block_bytes)),
        name="merge_proj",
    )(y_rnn, y_attn, w_rnn, w_attn, sm_rnn, sm_attn)


def _out_norm_kernel(m_ref, w_ref, x_ref, gate_ref, gain_ref, o_ref, res_ref):
    j = pl.program_id(1)
    n_j = pl.num_programs(1)
    tn = w_ref.shape[1]
    acc = jnp.dot(m_ref[...], w_ref[...], preferred_element_type=jnp.float32)
    res_ref[j] = x_ref[...] + gate_ref[...] * acc

    @pl.when(j == n_j - 1)
    def _():
        n_tiles = res_ref.shape[0]
        ss = jnp.sum(res_ref[0] * res_ref[0], axis=-1, keepdims=True)
        for t in range(1, n_tiles):
            ss = ss + jnp.sum(res_ref[t] * res_ref[t], axis=-1, keepdims=True)
        inv = lax.rsqrt(ss / (n_tiles * tn) + EPS)
        for t in range(n_tiles):
            o_ref[:, t * tn:(t + 1) * tn] = res_ref[t] * inv * gain_ref[:, t * tn:(t + 1) * tn]


def _out_norm(merged, w, x2, gate, gain):
    m, k = merged.shape
    n = w.shape[1]
    tm, tn = 512, 1024
    block_bytes = (2 * (tm * k * 2 + k * tn * 2 + tm * tn * 4 + tm * n * 4) + tm * n * 4
                   + tm * tn * 4)
    return pl.pallas_call(
        _out_norm_kernel,
        out_shape=jax.ShapeDtypeStruct((m, n), jnp.float32),
        grid=(m // tm, n // tn),
        in_specs=[pl.BlockSpec((tm, k), lambda i, j: (i, 0)),
                  pl.BlockSpec((k, tn), lambda i, j: (0, j)),
                  pl.BlockSpec((tm, tn), lambda i, j: (i, j)),
                  pl.BlockSpec((1, tn), lambda i, j: (0, j)),
                  pl.BlockSpec((1, n), lambda i, j: (0, 0))],
        out_specs=pl.BlockSpec((tm, n), lambda i, j: (i, 0)),
        scratch_shapes=[pltpu.VMEM((n // tn, tm, tn), jnp.float32)],
        compiler_params=pltpu.CompilerParams(
            dimension_semantics=("arbitrary", "arbitrary"),
            vmem_limit_bytes=_vmem_limit(block_bytes)),
        name="out_norm",
    )(merged, w, x2, gate, gain)


RNN_SUB_ROWS = 512
RNN_BLOCK_ROWS = 2048
RNN_HALO_GROUPS = 4
SQRT_FLOOR = 1e-30


def _slabs(ref, rows, n_slabs):
    return jnp.concatenate([ref[l, rows, :] for l in range(n_slabs)], axis=1)


def _rglru_kernel(u_ref, sg_ref, cw_ref, cb_ref, wa_ref, ba_ref, wx_ref, bx_ref, lam_ref,
                  cast_src, y_ref, cast_dst, uperm, hbuf, pbuf, carry_ref, *, conv_width):
    ts = RNN_SUB_ROWS
    tc = ts // SUBLANES
    width = u_ref.shape[1]
    n_slabs = width // LANES
    data0 = RNN_HALO_GROUPS * SUBLANES
    taps = conv_width - 1
    assert taps < RNN_HALO_GROUPS and taps <= tc
    _run_cast(cast_src, cast_dst)

    @pl.when(pl.program_id(1) == 0)
    def _():
        uperm[...] = jnp.zeros_like(uperm)
        carry_ref[...] = jnp.zeros_like(carry_ref)

    neg = -lam_ref[...]
    softplus_neg = jnp.maximum(neg, 0.0) + jnp.log1p(jnp.exp(-jnp.abs(neg)))
    half_decay = (-0.5 * LRU_C) * softplus_neg
    half_ba = 0.5 * ba_ref[...]
    half_bx = 0.5 * bx_ref[...]
    cw = cw_ref[...]
    sub = lax.broadcasted_iota(jnp.int32, (SUBLANES, width), 0)

    def group(j):
        return pl.ds(data0 + j * SUBLANES, SUBLANES)

    def sub_tile(s, _):
        r0 = pl.multiple_of(s * ts, ts)
        prev = [_slabs(uperm, group(tc - k), n_slabs) for k in range(1, taps + 1)]
        for c in range(SUBLANES):
            for l in range(n_slabs):
                uperm[l, pl.ds(data0 + c, tc, stride=SUBLANES), :] = (
                    u_ref[pl.ds(r0 + c * tc, tc), l * LANES:(l + 1) * LANES])
        for k in range(1, taps + 1):
            cur = _slabs(uperm, group(tc - k), n_slabs)
            halo = jnp.where(sub == 0, pltpu.roll(prev[k - 1], 1, axis=0),
                             pltpu.roll(cur, 1, axis=0))
            for l in range(n_slabs):
                uperm[l, group(-k), :] = halo[:, l * LANES:(l + 1) * LANES]
        u = cb_ref[...]
        for k in range(conv_width):
            first = data0 - (taps - k) * SUBLANES
            u = u + cw[k:k + 1, :] * _slabs(uperm, pl.ds(first, ts), n_slabs)
        ub = u.astype(jnp.bfloat16)
        t_r = jnp.tanh(jnp.dot(ub, wa_ref[...], preferred_element_type=jnp.float32) + half_ba)
        t_i = jnp.tanh(jnp.dot(ub, wx_ref[...], preferred_element_type=jnp.float32) + half_bx)
        log_a = half_decay * t_r + half_decay
        a = jnp.exp(log_a)
        half_u = 0.5 * u
        gated_u = half_u * t_i + half_u
        one_m_a2 = 1.0 - a * a
        b = (one_m_a2 * lax.rsqrt(jnp.maximum(one_m_a2, SQRT_FLOOR))) * gated_u

        h_loc = b[0:SUBLANES]
        p_cum = a[0:SUBLANES]
        for j in range(tc):
            if j > 0:
                aj = a[j * SUBLANES:(j + 1) * SUBLANES]
                h_loc = aj * h_loc + b[j * SUBLANES:(j + 1) * SUBLANES]
                p_cum = aj * p_cum
            pbuf[j * SUBLANES:(j + 1) * SUBLANES, :] = p_cum
            for l in range(n_slabs):
                hbuf[l, j * SUBLANES:(j + 1) * SUBLANES, :] = h_loc[:, l * LANES:(l + 1) * LANES]

        a_inc, b_inc = p_cum, h_loc
        shift = 1
        while shift < SUBLANES:
            keep = sub >= shift
            a_prev = jnp.where(keep, pltpu.roll(a_inc, shift, axis=0), 1.0)
            b_prev = jnp.where(keep, pltpu.roll(b_inc, shift, axis=0), 0.0)
            b_inc = a_inc * b_prev + b_inc
            a_inc = a_inc * a_prev
            shift *= 2
        h_in = jnp.broadcast_to(carry_ref[...], (SUBLANES, width))
        chunk_end = b_inc + a_inc * h_in
        entry = jnp.where(sub == 0, h_in, pltpu.roll(chunk_end, 1, axis=0))
        carry_ref[...] = chunk_end[SUBLANES - 1:SUBLANES, :]

        for j in range(tc):
            rows = slice(j * SUBLANES, (j + 1) * SUBLANES)
            h = _slabs(hbuf, rows, n_slabs) + pbuf[rows, :] * entry
            for l in range(n_slabs):
                hbuf[l, rows, :] = h[:, l * LANES:(l + 1) * LANES]
        for c in range(SUBLANES):
            rows = pl.ds(r0 + c * tc, tc)
            h_nat = _slabs(hbuf, pl.ds(c, tc, stride=SUBLANES), n_slabs)
            y_ref[rows, :] = (h_nat * sg_ref[rows, :]).astype(y_ref.dtype)
        return 0

    lax.fori_loop(0, u_ref.shape[0] // ts, sub_tile, 0, unroll=2)


def _rglru(u_rnn, sg_rnn, conv_w, conv_b, w_gate_a, b_gate_a, w_gate_x, b_gate_x, lam, cast):
    s, d = u_rnn.shape
    nb, bw, _ = w_gate_a.shape
    conv_width = conv_w.shape[0]
    tb = RNN_BLOCK_ROWS
    seq = pl.BlockSpec((tb, bw), lambda c, t: (t, c))
    vec = pl.BlockSpec((1, bw), lambda c, t: (0, c))
    gate_w = pl.BlockSpec((None, bw, bw), lambda c, t: (c, 0, 0))
    gate_b = pl.BlockSpec((None, 1, bw), lambda c, t: (c, 0, 0))
    block_bytes = (2 * tb * bw * (4 + 4 + 2) + 4 * bw * bw * 2
                   + 3 * (RNN_SUB_ROWS + RNN_HALO_GROUPS * SUBLANES) * bw * 4)
    grid = (nb, s // tb)
    c_in, c_out, c_shape, c_bytes = _cast_specs(cast, grid)
    return pl.pallas_call(
        functools.partial(_rglru_kernel, conv_width=conv_width),
        out_shape=(jax.ShapeDtypeStruct((s, d), jnp.bfloat16), c_shape),
        grid=grid,
        in_specs=[seq, seq,
                  pl.BlockSpec((conv_width, bw), lambda c, t: (0, c)), vec,
                  gate_w, gate_b, gate_w, gate_b, vec, c_in],
        out_specs=(seq, c_out),
        scratch_shapes=[
            pltpu.VMEM((bw // LANES, RNN_SUB_ROWS + RNN_HALO_GROUPS * SUBLANES, LANES),
                       jnp.float32),
            pltpu.VMEM((bw // LANES, RNN_SUB_ROWS, LANES), jnp.float32),
            pltpu.VMEM((RNN_SUB_ROWS, bw), jnp.float32),
            pltpu.VMEM((1, bw), jnp.float32)],
        compiler_params=pltpu.CompilerParams(
            dimension_semantics=("arbitrary", "arbitrary"),
            vmem_limit_bytes=_vmem_limit(block_bytes + c_bytes + 16 * 1024 * 1024)),
        name="rglru",
    )(u_rnn, sg_rnn, conv_w, conv_b.reshape(1, d),
      (0.5 * w_gate_a).astype(jnp.bfloat16), b_gate_a.reshape(nb, 1, bw),
      (0.5 * w_gate_x).astype(jnp.bfloat16), b_gate_x.reshape(nb, 1, bw), lam.reshape(1, d),
      cast.src)


def _swap_halves(v):
    half = v.shape[1] // 2
    return jnp.concatenate([v[:, half:], v[:, :half]], axis=1)


def _attn_block(first, sink_ref, q_ref, k_prev, k_cur, sg_ref, o_ref, *, n_kv, group, head_dim):
    bq = q_ref.shape[0]
    pair_w = 2 * head_dim
    pairs_per_group = group // 2
    d_kv = n_kv * head_dim

    q_pos = lax.broadcasted_iota(jnp.int32, (bq, 2 * bq), 0)
    k_pos = lax.broadcasted_iota(jnp.int32, (bq, 2 * bq), 1)
    dist = q_pos + bq - k_pos
    valid = (dist >= 0) & (dist < WINDOW) & ((k_pos >= bq) | jnp.logical_not(first))
    lane = lax.broadcasted_iota(jnp.int32, (bq, pair_w), 1)
    lo_q = lane < head_dim
    lane_kv = lax.broadcasted_iota(jnp.int32, (2 * bq, pair_w), 1)
    lo_kv = lane_kv < head_dim
    scale = head_dim ** -0.5
    zero_q = jnp.zeros((bq, pair_w), q_ref.dtype)

    for kv_pair in range(n_kv // 2):
        cols = slice(kv_pair * pair_w, (kv_pair + 1) * pair_w)
        k_two = jnp.concatenate([k_prev[:, cols], k_cur[:, cols]], axis=0)
        v_cols = slice(d_kv + kv_pair * pair_w, d_kv + (kv_pair + 1) * pair_w)
        v_two = jnp.concatenate([k_prev[:, v_cols], k_cur[:, v_cols]], axis=0)
        k_swapped = _swap_halves(k_two)
        v_swapped = _swap_halves(v_two)
        for parity in range(2):
            hk = 2 * kv_pair + parity
            if parity == 0:
                k_dup = jnp.where(lo_kv, k_two, k_swapped)
                v_dup = jnp.where(lo_kv, v_two, v_swapped)
            else:
                k_dup = jnp.where(lo_kv, k_swapped, k_two)
                v_dup = jnp.where(lo_kv, v_swapped, v_two)
            k_dup = k_dup * jnp.asarray(scale, k_dup.dtype)
            q_rows = []
            for p in range(pairs_per_group):
                pair = hk * pairs_per_group + p
                qp = q_ref[:, pair * pair_w:(pair + 1) * pair_w]
                q_rows.append(jnp.where(lo_q, qp, zero_q))
                q_rows.append(jnp.where(lo_q, zero_q, qp))
            q_stack = jnp.concatenate(q_rows, axis=0)
            s_all = lax.dot_general(q_stack, k_dup, (((1,), (1,)), ((), ())),
                                    preferred_element_type=jnp.float32)
            p_rows, inv_rows = [], []
            for g in range(group):
                sink = sink_ref[hk * group + g]
                sc = jnp.where(valid, s_all[g * bq:(g + 1) * bq, :], -jnp.inf)
                m = jnp.maximum(jnp.max(sc, axis=-1, keepdims=True), sink)
                e = jnp.exp(sc - m)
                denom = jnp.sum(e, axis=-1, keepdims=True) + jnp.exp(sink - m)
                p_rows.append(e.astype(v_dup.dtype))
                inv_rows.append(1.0 / denom)
            p_stack = jnp.concatenate(p_rows, axis=0)
            o_all = jnp.dot(p_stack, v_dup, preferred_element_type=jnp.float32)
            for p in range(pairs_per_group):
                pair = hk * pairs_per_group + p
                o_even = o_all[(2 * p) * bq:(2 * p + 1) * bq, :] * inv_rows[2 * p]
                o_odd = o_all[(2 * p + 1) * bq:(2 * p + 2) * bq, :] * inv_rows[2 * p + 1]
                o_pair = jnp.where(lo_q, o_even, o_odd)
                c0 = pair * pair_w
                o_ref[:, c0:c0 + pair_w] = (o_pair * sg_ref[:, c0:c0 + pair_w]).astype(o_ref.dtype)


def _attn_kernel(sink_ref, q_ref, kv_ref, kvp_ref, sg_ref, cast_src, o_ref, cast_dst, *,
                 n_kv, group, head_dim):
    _run_cast(cast_src, cast_dst)
    bq = BLOCK_Q
    n = pl.program_id(0)
    for b in range(q_ref.shape[0] // bq):
        rows = pl.ds(b * bq, bq)
        k_prev = kvp_ref if b == 0 else kv_ref.at[pl.ds((b - 1) * bq, bq), :]
        first = (n == 0) if b == 0 else jnp.bool_(False)
        _attn_block(first, sink_ref, q_ref.at[rows, :], k_prev, kv_ref.at[rows, :],
                    sg_ref.at[rows, :], o_ref.at[rows, :], n_kv=n_kv, group=group,
                    head_dim=head_dim)


def _attention(q, kv, sg_attn, sinks, cast, *, n_kv, head_dim):
    s, d_attn = q.shape
    n_q = d_attn // head_dim
    group = n_q // n_kv
    assert head_dim * 2 == LANES and group % 2 == 0 and n_kv % 2 == 0
    bq = BLOCK_Q
    rows = ATTN_BLOCKS_PER_STEP * bq
    d_kv2 = kv.shape[1]
    grid = (s // rows, 1)
    c_in, c_out, c_shape, c_bytes = _cast_specs(cast, grid)
    block_bytes = 2 * rows * (d_attn * (2 + 4 + 2) + d_kv2 * 2) + 2 * bq * d_kv2 * 2 + c_bytes
    seq = pl.BlockSpec((rows, d_attn), lambda n, _: (n, 0))
    return pl.pallas_call(
        functools.partial(_attn_kernel, n_kv=n_kv, group=group, head_dim=head_dim),
        out_shape=(jax.ShapeDtypeStruct((s, d_attn), jnp.bfloat16), c_shape),
        grid=grid,
        in_specs=[pl.BlockSpec(memory_space=pltpu.SMEM),
                  seq,
                  pl.BlockSpec((rows, d_kv2), lambda n, _: (n, 0)),
                  pl.BlockSpec((bq, d_kv2),
                               lambda n, _: (jnp.maximum(n * ATTN_BLOCKS_PER_STEP - 1, 0), 0)),
                  seq, c_in],
        out_specs=(seq, c_out),
        compiler_params=pltpu.CompilerParams(
            dimension_semantics=("arbitrary", "arbitrary"),
            vmem_limit_bytes=_vmem_limit(block_bytes + 16 * 1024 * 1024)),
        name="swa_attn",
    )(sinks, q, kv, kv, sg_attn, cast.src)


def kernel(x, c, w_ada, b_ada, norm_gain, w_in, conv_w, conv_b, w_gate_a, b_gate_a, w_gate_x,
           b_gate_x, lru_lambda, attn_sinks, w_rnn_proj, w_attn_proj, w_out, final_norm_gain):
    bsz, seq, d = x.shape
    depth = w_in.shape[0]
    d_rnn = w_rnn_proj.shape[1]
    d_attn = w_attn_proj.shape[1]
    n_q = attn_sinks.shape[1]
    head_dim = d_attn // n_q
    d_kv = (w_in.shape[2] - 2 * d_rnn - 2 * d_attn - 2 * d) // 2
    n_kv = d_kv // head_dim
    assert bsz == 1

    xs = x.reshape(bsz * seq, d)
    bf = jnp.bfloat16
    for l in range(depth):
        mod = _modulation(c, w_ada[l], b_ada[l])
        shift, scale, gate = mod[:, :d], mod[:, d:2 * d], mod[:, 2 * d:]
        h = _norm_mod(xs, norm_gain[l].reshape(1, d), scale, shift)

        wl = w_in[l]
        c_u, c_g, c_q, c_kv = 0, d_rnn, 2 * d_rnn, 2 * d_rnn + d_attn
        c_ga = c_kv + 2 * d_kv
        c_mr, c_ma = c_ga + d_attn, c_ga + d_attn + d
        proj = functools.partial(_matmul, h)
        w_kv = _cast_cols(CastJob(wl, c_kv, 2 * d_kv))
        w_u = _cast_cols(CastJob(wl, c_u, d_rnn))
        kv = proj(w_kv, out_dtype=bf, name="proj_kv")
        u_rnn, w_g = proj(w_u, out_dtype=jnp.float32, cast=CastJob(wl, c_g, d_rnn),
                          name="proj_u")
        sg_rnn, w_q = proj(w_g, out_dtype=jnp.float32, epilogue=_ep_silu,
                           cast=CastJob(wl, c_q, d_attn), name="proj_g_rnn")
        q, w_ga = proj(w_q, out_dtype=bf, cast=CastJob(wl, c_ga, d_attn), name="proj_q")
        sg_attn, w_mr = proj(w_ga, out_dtype=jnp.float32, epilogue=_ep_silu,
                             cast=CastJob(wl, c_mr, d), name="proj_g_attn")
        sm_rnn, w_ma = proj(w_mr, out_dtype=jnp.float32, epilogue=_ep_sigmoid,
                            cast=CastJob(wl, c_ma, d), name="proj_m_rnn")
        sm_attn, w_rp = proj(w_ma, out_dtype=jnp.float32, epilogue=_ep_sigmoid,
                             cast=CastJob(w_rnn_proj[l], 0, d), name="proj_m_attn")

        y_rnn, w_ap = _rglru(u_rnn, sg_rnn, conv_w[l], conv_b[l], w_gate_a[l], b_gate_a[l],
                             w_gate_x[l], b_gate_x[l], lru_lambda[l],
                             CastJob(w_attn_proj[l], 0, d))
        y_attn, w_o = _attention(q, kv, sg_attn, attn_sinks[l], CastJob(w_out[l], 0, d),
                                 n_kv=n_kv, head_dim=head_dim)

        merged = _merge(y_rnn, y_attn, w_rp, w_ap, sm_rnn, sm_attn)
        if l + 1 < depth:
            xs = _matmul(merged, w_o, out_dtype=jnp.float32, epilogue=_ep_residual,
                         tiles=(xs,), rows=(gate,), name="out_proj")
        else:
            xs = _out_norm(merged, w_o, xs, gate, final_norm_gain.reshape(1, d))
    return xs.reshape(bsz, seq, d)
```

```python
import functools
from typing import NamedTuple

import jax
import jax.numpy as jnp
from jax import lax
from jax.experimental import pallas as pl
from jax.experimental.pallas import tpu as pltpu

EPS = 1e-6
LRU_C = 8.0
WINDOW = 128
BLOCK_Q = 128
ATTN_BLOCKS_PER_STEP = 2

V7X_VMEM_BYTES = 64 * 1024 * 1024
VMEM_HEADROOM_BYTES = 6 * 1024 * 1024
LANES = 128
SUBLANES = 8


def _vmem_limit(block_bytes):
    need = int(block_bytes) + VMEM_HEADROOM_BYTES
    assert need <= V7X_VMEM_BYTES - 2 * 1024 * 1024, need
    return need


def _sigmoid(v):
    return 0.5 * jnp.tanh(0.5 * v) + 0.5


def _silu(v):
    return v * _sigmoid(v)


def _mod_kernel(c_ref, w_ref, b_ref, o_ref):
    c_act = _silu(c_ref[...])
    o_ref[...] = jnp.sum(w_ref[...] * c_act, axis=0, keepdims=True) + b_ref[...]


def _modulation(c, w_ada, b_ada):
    d, n = w_ada.shape
    tn = 512
    return pl.pallas_call(
        _mod_kernel,
        out_shape=jax.ShapeDtypeStruct((1, n), jnp.float32),
        grid=(n // tn,),
        in_specs=[pl.BlockSpec((d, 1), lambda j: (0, 0)),
                  pl.BlockSpec((d, tn), lambda j: (0, j)),
                  pl.BlockSpec((1, tn), lambda j: (0, j))],
        out_specs=pl.BlockSpec((1, tn), lambda j: (0, j)),
        compiler_params=pltpu.CompilerParams(
            dimension_semantics=("arbitrary",),
            vmem_limit_bytes=_vmem_limit(2 * d * tn * 4 + 2 * d * LANES * 4 + d * tn * 4)),
        name="adaln_mod",
    )(c.reshape(d, 1), w_ada, b_ada.reshape(1, n))


def _norm_mod_kernel(x_ref, gain_ref, scale_ref, shift_ref, o_ref):
    x = x_ref[...]
    ms = jnp.mean(x * x, axis=-1, keepdims=True)
    g = gain_ref[...] * (1.0 + scale_ref[...])
    o_ref[...] = (x * lax.rsqrt(ms + EPS) * g + shift_ref[...]).astype(o_ref.dtype)


def _norm_mod(x2, gain, scale, shift):
    s, d = x2.shape
    tm = 512
    row = pl.BlockSpec((1, d), lambda i: (0, 0))
    return pl.pallas_call(
        _norm_mod_kernel,
        out_shape=jax.ShapeDtypeStruct((s, d), jnp.bfloat16),
        grid=(s // tm,),
        in_specs=[pl.BlockSpec((tm, d), lambda i: (i, 0)), row, row, row],
        out_specs=pl.BlockSpec((tm, d), lambda i: (i, 0)),
        compiler_params=pltpu.CompilerParams(
            dimension_semantics=("arbitrary",),
            vmem_limit_bytes=_vmem_limit(2 * tm * d * (4 + 2) + 2 * tm * d * 4)),
        name="norm_mod",
    )(x2, gain, scale, shift)


CAST_BLOCK_COLS = 1024


class CastJob(NamedTuple):
    src: jax.Array
    col0: int
    cols: int


def _cast_specs(job, grid):
    g0, g1 = grid
    n_steps = g0 * g1
    rows = job.src.shape[0]
    cw = min(job.cols, CAST_BLOCK_COLS)
    n_cb = job.cols // cw
    rb = rows * n_cb // n_steps
    assert rb * n_steps == rows * n_cb and rb % 16 == 0 and job.col0 % cw == 0
    col_blk0 = job.col0 // cw

    def blk(i, j):
        s = i * g1 + j
        return s // n_cb, s % n_cb

    in_spec = pl.BlockSpec((rb, cw), lambda i, j: (blk(i, j)[0], col_blk0 + blk(i, j)[1]))
    out_spec = pl.BlockSpec((rb, cw), lambda i, j: blk(i, j))
    out_shape = jax.ShapeDtypeStruct((rows, job.cols), jnp.bfloat16)
    return in_spec, out_spec, out_shape, 2 * rb * cw * (4 + 2)


def _run_cast(src_ref, dst_ref):
    dst_ref[...] = src_ref[...].astype(dst_ref.dtype)


def _cast_cols(job):
    rows = job.src.shape[0]
    rb = 512
    cw = min(job.cols, CAST_BLOCK_COLS)
    assert rows % rb == 0 and job.cols % cw == 0 and job.col0 % cw == 0
    col_blk0 = job.col0 // cw
    return pl.pallas_call(
        _run_cast,
        out_shape=jax.ShapeDtypeStruct((rows, job.cols), jnp.bfloat16),
        grid=(rows // rb, job.cols // cw),
        in_specs=[pl.BlockSpec((rb, cw), lambda i, j: (i, col_blk0 + j))],
        out_specs=pl.BlockSpec((rb, cw), lambda i, j: (i, j)),
        compiler_params=pltpu.CompilerParams(
            dimension_semantics=("arbitrary", "arbitrary"),
            vmem_limit_bytes=_vmem_limit(2 * rb * cw * (4 + 2))),
        name="cast_cols",
    )(job.src)


def _ep_identity(acc):
    return acc


def _ep_silu(acc):
    return _silu(acc)


def _ep_sigmoid(acc):
    return _sigmoid(acc)


def _ep_residual(acc, x, gate):
    return x + gate * acc


def _matmul_kernel(*refs, epilogue, n_tiles, n_rows, has_cast):
    lhs_ref, rhs_ref = refs[0], refs[1]
    n_in = 2 + n_tiles + n_rows
    tile_refs = refs[2:2 + n_tiles]
    row_refs = refs[2 + n_tiles:n_in]
    o_ref = refs[n_in + has_cast]
    if has_cast:
        _run_cast(refs[n_in], refs[n_in + 2])
    acc = jnp.dot(lhs_ref[...], rhs_ref[...], preferred_element_type=jnp.float32)
    extras = [r[...] for r in tile_refs] + [r[...] for r in row_refs]
    o_ref[...] = epilogue(acc, *extras).astype(o_ref.dtype)


def _matmul(lhs, rhs, *, out_dtype, epilogue=_ep_identity, tiles=(), rows=(), tn=1024,
            cast=None, name):
    m, k = lhs.shape
    n_cols = rhs.shape[1]
    tm = 512 if tiles else 1024
    assert rhs.shape[0] == k and n_cols % tn == 0 and m % tm == 0
    grid = (m // tm, n_cols // tn)
    in_specs = [pl.BlockSpec((tm, k), lambda i, j: (i, 0)),
                pl.BlockSpec((k, tn), lambda i, j: (0, j))]
    in_specs += [pl.BlockSpec((tm, tn), lambda i, j: (i, j)) for _ in tiles]
    in_specs += [pl.BlockSpec((1, tn), lambda i, j: (0, j)) for _ in rows]
    out_bytes = jnp.dtype(out_dtype).itemsize
    tile_bytes = sum(t.dtype.itemsize for t in tiles)
    block_bytes = 2 * (tm * k * lhs.dtype.itemsize + k * tn * rhs.dtype.itemsize
                       + tm * tn * (out_bytes + tile_bytes)) + 2 * tm * tn * 4
    out_shape = jax.ShapeDtypeStruct((m, n_cols), out_dtype)
    out_spec = pl.BlockSpec((tm, tn), lambda i, j: (i, j))
    args = (lhs, rhs, *tiles, *rows)
    if cast is not None:
        c_in, c_out, c_shape, c_bytes = _cast_specs(cast, grid)
        in_specs.append(c_in)
        out_shape, out_spec = (out_shape, c_shape), (out_spec, c_out)
        args += (cast.src,)
        block_bytes += c_bytes
    return pl.pallas_call(
        functools.partial(_matmul_kernel, epilogue=epilogue, n_tiles=len(tiles),
                          n_rows=len(rows), has_cast=cast is not None),
        out_shape=out_shape,
        grid=grid,
        in_specs=in_specs,
        out_specs=out_spec,
        compiler_params=pltpu.CompilerParams(
            dimension_semantics=("arbitrary", "arbitrary"),
            vmem_limit_bytes=_vmem_limit(block_bytes)),
        name=name,
    )(*args)


def _merge_kernel(yr_ref, ya_ref, wr_ref, wa_ref, smr_ref, sma_ref, o_ref):
    p_rnn = jnp.dot(yr_ref[...], wr_ref[...], preferred_element_type=jnp.float32)
    p_attn = jnp.dot(ya_ref[...], wa_ref[...], preferred_element_type=jnp.float32)
    o_ref[...] = (smr_ref[...] * p_rnn + sma_ref[...] * p_attn).astype(o_ref.dtype)


def _merge(y_rnn, y_attn, w_rnn, w_attn, sm_rnn, sm_attn):
    m, k = y_rnn.shape
    n = w_rnn.shape[1]
    tm, tn = 1024, 512
    lhs = pl.BlockSpec((tm, k), lambda j, i: (i, 0))
    rhs = pl.BlockSpec((k, tn), lambda j, i: (0, j), pipeline_mode=pl.Buffered(1))
    tile = pl.BlockSpec((tm, tn), lambda j, i: (i, j))
    block_bytes = (2 * (2 * tm * k * 2 + tm * tn * (4 + 4 + 2)) + 2 * k * tn * 2
                   + 2 * tm * tn * 4)
    return pl.pallas_call(
        _merge_kernel,
        out_shape=jax.ShapeDtypeStruct((m, n), jnp.bfloat16),
        grid=(n // tn, m // tm),
        in_specs=[lhs, lhs, rhs, rhs, tile, tile],
        out_specs=tile,
        compiler_params=pltpu.CompilerParams(
            dimension_semantics=("arbitrary", "arbitrary"),
            vmem_limit_bytes=_vmem_limit(block_bytes)),
        name="merge_proj",
    )(y_rnn, y_attn, w_rnn, w_attn, sm_rnn, sm_attn)


def _out_norm_kernel(m_ref, w_ref, x_ref, gate_ref, gain_ref, o_ref, res_ref):
    j = pl.program_id(1)
    n_j = pl.num_programs(1)
    tn = w_ref.shape[1]
    acc = jnp.dot(m_ref[...], w_ref[...], preferred_element_type=jnp.float32)
    res_ref[j] = x_ref[...] + gate_ref[...] * acc

    @pl.when(j == n_j - 1)
    def _():
        n_tiles = res_ref.shape[0]
        ss = jnp.sum(res_ref[0] * res_ref[0], axis=-1, keepdims=True)
        for t in range(1, n_tiles):
            ss = ss + jnp.sum(res_ref[t] * res_ref[t], axis=-1, keepdims=True)
        inv = lax.rsqrt(ss / (n_tiles * tn) + EPS)
        for t in range(n_tiles):
            o_ref[:, t * tn:(t + 1) * tn] = res_ref[t] * inv * gain_ref[:, t * tn:(t + 1) * tn]


def _out_norm(merged, w, x2, gate, gain):
    m, k = merged.shape
    n = w.shape[1]
    tm, tn = 512, 1024
    block_bytes = (2 * (tm * k * 2 + k * tn * 2 + tm * tn * 4 + tm * n * 4) + tm * n * 4
                   + tm * tn * 4)
    return pl.pallas_call(
        _out_norm_kernel,
        out_shape=jax.ShapeDtypeStruct((m, n), jnp.float32),
        grid=(m // tm, n // tn),
        in_specs=[pl.BlockSpec((tm, k), lambda i, j: (i, 0)),
                  pl.BlockSpec((k, tn), lambda i, j: (0, j)),
                  pl.BlockSpec((tm, tn), lambda i, j: (i, j)),
                  pl.BlockSpec((1, tn), lambda i, j: (0, j)),
                  pl.BlockSpec((1, n), lambda i, j: (0, 0))],
        out_specs=pl.BlockSpec((tm, n), lambda i, j: (i, 0)),
        scratch_shapes=[pltpu.VMEM((n // tn, tm, tn), jnp.float32)],
        compiler_params=pltpu.CompilerParams(
            dimension_semantics=("arbitrary", "arbitrary"),
            vmem_limit_bytes=_vmem_limit(block_bytes)),
        name="out_norm",
    )(merged, w, x2, gate, gain)


RNN_SUB_ROWS = 512
RNN_BLOCK_ROWS = 2048
RNN_HALO_GROUPS = 4
SQRT_FLOOR = 1e-30


def _slabs(ref, rows, n_slabs):
    return jnp.concatenate([ref[l, rows, :] for l in range(n_slabs)], axis=1)


def _rglru_kernel(u_ref, sg_ref, cw_ref, cb_ref, wa_ref, ba_ref, wx_ref, bx_ref, lam_ref,
                  cast_src, y_ref, cast_dst, uperm, hbuf, pbuf, carry_ref, *, conv_width):
    ts = RNN_SUB_ROWS
    tc = ts // SUBLANES
    width = u_ref.shape[1]
    n_slabs = width // LANES
    data0 = RNN_HALO_GROUPS * SUBLANES
    taps = conv_width - 1
    assert taps < RNN_HALO_GROUPS and taps <= tc
    _run_cast(cast_src, cast_dst)

    @pl.when(pl.program_id(1) == 0)
    def _():
        uperm[...] = jnp.zeros_like(uperm)
        carry_ref[...] = jnp.zeros_like(carry_ref)

    neg = -lam_ref[...]
    softplus_neg = jnp.maximum(neg, 0.0) + jnp.log1p(jnp.exp(-jnp.abs(neg)))
    half_decay = (-0.5 * LRU_C) * softplus_neg
    half_ba = 0.5 * ba_ref[...]
    half_bx = 0.5 * bx_ref[...]
    cw = cw_ref[...]
    sub = lax.broadcasted_iota(jnp.int32, (SUBLANES, width), 0)

    def group(j):
        return pl.ds(data0 + j * SUBLANES, SUBLANES)

    def sub_tile(s, _):
        r0 = pl.multiple_of(s * ts, ts)
        prev = [_slabs(uperm, group(tc - k), n_slabs) for k in range(1, taps + 1)]
        for c in range(SUBLANES):
            for l in range(n_slabs):
                uperm[l, pl.ds(data0 + c, tc, stride=SUBLANES), :] = (
                    u_ref[pl.ds(r0 + c * tc, tc), l * LANES:(l + 1) * LANES])
        for k in range(1, taps + 1):
            cur = _slabs(uperm, group(tc - k), n_slabs)
            halo = jnp.where(sub == 0, pltpu.roll(prev[k - 1], 1, axis=0),
                             pltpu.roll(cur, 1, axis=0))
            for l in range(n_slabs):
                uperm[l, group(-k), :] = halo[:, l * LANES:(l + 1) * LANES]
        u = cb_ref[...]
        for k in range(conv_width):
            first = data0 - (taps - k) * SUBLANES
            u = u + cw[k:k + 1, :] * _slabs(uperm, pl.ds(first, ts), n_slabs)
        ub = u.astype(jnp.bfloat16)
        t_r = jnp.tanh(jnp.dot(ub, wa_ref[...], preferred_element_type=jnp.float32) + half_ba)
        t_i = jnp.tanh(jnp.dot(ub, wx_ref[...], preferred_element_type=jnp.float32) + half_bx)
        log_a = half_decay * t_r + half_decay
        a = jnp.exp(log_a)
        half_u = 0.5 * u
        gated_u = half_u * t_i + half_u
        one_m_a2 = 1.0 - a * a
        b = (one_m_a2 * lax.rsqrt(jnp.maximum(one_m_a2, SQRT_FLOOR))) * gated_u

        h_loc = b[0:SUBLANES]
        p_cum = a[0:SUBLANES]
        for j in range(tc):
            if j > 0:
                aj = a[j * SUBLANES:(j + 1) * SUBLANES]
                h_loc = aj * h_loc + b[j * SUBLANES:(j + 1) * SUBLANES]
                p_cum = aj * p_cum
            pbuf[j * SUBLANES:(j + 1) * SUBLANES, :] = p_cum
            for l in range(n_slabs):
                hbuf[l, j * SUBLANES:(j + 1) * SUBLANES, :] = h_loc[:, l * LANES:(l + 1) * LANES]

        a_inc, b_inc = p_cum, h_loc
        shift = 1
        while shift < SUBLANES:
            keep = sub >= shift
            a_prev = jnp.where(keep, pltpu.roll(a_inc, shift, axis=0), 1.0)
            b_prev = jnp.where(keep, pltpu.roll(b_inc, shift, axis=0), 0.0)
            b_inc = a_inc * b_prev + b_inc
            a_inc = a_inc * a_prev
            shift *= 2
        h_in = jnp.broadcast_to(carry_ref[...], (SUBLANES, width))
        chunk_end = b_inc + a_inc * h_in
        entry = jnp.where(sub == 0, h_in, pltpu.roll(chunk_end, 1, axis=0))
        carry_ref[...] = chunk_end[SUBLANES - 1:SUBLANES, :]

        for j in range(tc):
            rows = slice(j * SUBLANES, (j + 1) * SUBLANES)
            h = _slabs(hbuf, rows, n_slabs) + pbuf[rows, :] * entry
            for l in range(n_slabs):
                hbuf[l, rows, :] = h[:, l * LANES:(l + 1) * LANES]
        for c in range(SUBLANES):
            rows = pl.ds(r0 + c * tc, tc)
            h_nat = _slabs(hbuf, pl.ds(c, tc, stride=SUBLANES), n_slabs)
            y_ref[rows, :] = (h_nat * sg_ref[rows, :]).astype(y_ref.dtype)
        return 0

    lax.fori_loop(0, u_ref.shape[0] // ts, sub_tile, 0, unroll=2)


def _rglru(u_rnn, sg_rnn, conv_w, conv_b, w_gate_a, b_gate_a, w_gate_x, b_gate_x, lam, cast):
    s, d = u_rnn.shape
    nb, bw, _ = w_gate_a.shape
    conv_width = conv_w.shape[0]
    tb = RNN_BLOCK_ROWS
    seq = pl.BlockSpec((tb, bw), lambda c, t: (t, c))
    vec = pl.BlockSpec((1, bw), lambda c, t: (0, c))
    gate_w = pl.BlockSpec((None, bw, bw), lambda c, t: (c, 0, 0))
    gate_b = pl.BlockSpec((None, 1, bw), lambda c, t: (c, 0, 0))
    block_bytes = (2 * tb * bw * (4 + 4 + 2) + 4 * bw * bw * 2
                   + 3 * (RNN_SUB_ROWS + RNN_HALO_GROUPS * SUBLANES) * bw * 4)
    grid = (nb, s // tb)
    c_in, c_out, c_shape, c_bytes = _cast_specs(cast, grid)
    return pl.pallas_call(
        functools.partial(_rglru_kernel, conv_width=conv_width),
        out_shape=(jax.ShapeDtypeStruct((s, d), jnp.bfloat16), c_shape),
        grid=grid,
        in_specs=[seq, seq,
                  pl.BlockSpec((conv_width, bw), lambda c, t: (0, c)), vec,
                  gate_w, gate_b, gate_w, gate_b, vec, c_in],
        out_specs=(seq, c_out),
        scratch_shapes=[
            pltpu.VMEM((bw // LANES, RNN_SUB_ROWS + RNN_HALO_GROUPS * SUBLANES, LANES),
                       jnp.float32),
            pltpu.VMEM((bw // LANES, RNN_SUB_ROWS, LANES), jnp.float32),
            pltpu.VMEM((RNN_SUB_ROWS, bw), jnp.float32),
            pltpu.VMEM((1, bw), jnp.float32)],
        compiler_params=pltpu.CompilerParams(
            dimension_semantics=("arbitrary", "arbitrary"),
            vmem_limit_bytes=_vmem_limit(block_bytes + c_bytes + 16 * 1024 * 1024)),
        name="rglru",
    )(u_rnn, sg_rnn, conv_w, conv_b.reshape(1, d),
      (0.5 * w_gate_a).astype(jnp.bfloat16), b_gate_a.reshape(nb, 1, bw),
      (0.5 * w_gate_x).astype(jnp.bfloat16), b_gate_x.reshape(nb, 1, bw), lam.reshape(1, d),
      cast.src)


def _swap_halves(v):
    half = v.shape[1] // 2
    return jnp.concatenate([v[:, half:], v[:, :half]], axis=1)


def _attn_block(first, sink_ref, q_ref, k_prev, k_cur, sg_ref, o_ref, *, n_kv, group, head_dim):
    bq = q_ref.shape[0]
    pair_w = 2 * head_dim
    pairs_per_group = group // 2
    d_kv = n_kv * head_dim

    q_pos = lax.broadcasted_iota(jnp.int32, (bq, 2 * bq), 0)
    k_pos = lax.broadcasted_iota(jnp.int32, (bq, 2 * bq), 1)
    dist = q_pos + bq - k_pos
    valid = (dist >= 0) & (dist < WINDOW) & ((k_pos >= bq) | jnp.logical_not(first))
    lane = lax.broadcasted_iota(jnp.int32, (bq, pair_w), 1)
    lo_q = lane < head_dim
    lane_kv = lax.broadcasted_iota(jnp.int32, (2 * bq, pair_w), 1)
    lo_kv = lane_kv < head_dim
    scale = head_dim ** -0.5
    zero_q = jnp.zeros((bq, pair_w), q_ref.dtype)

    for kv_pair in range(n_kv // 2):
        cols = slice(kv_pair * pair_w, (kv_pair + 1) * pair_w)
        k_two = jnp.concatenate([k_prev[:, cols], k_cur[:, cols]], axis=0)
        v_cols = slice(d_kv + kv_pair * pair_w, d_kv + (kv_pair + 1) * pair_w)
        v_two = jnp.concatenate([k_prev[:, v_cols], k_cur[:, v_cols]], axis=0)
        k_swapped = _swap_halves(k_two)
        v_swapped = _swap_halves(v_two)
        for parity in range(2):
            hk = 2 * kv_pair + parity
            if parity == 0:
                k_dup = jnp.where(lo_kv, k_two, k_swapped)
                v_dup = jnp.where(lo_kv, v_two, v_swapped)
            else:
                k_dup = jnp.where(lo_kv, k_swapped, k_two)
                v_dup = jnp.where(lo_kv, v_swapped, v_two)
            k_dup = k_dup * jnp.asarray(scale, k_dup.dtype)
            q_rows = []
            for p in range(pairs_per_group):
                pair = hk * pairs_per_group + p
                qp = q_ref[:, pair * pair_w:(pair + 1) * pair_w]
                q_rows.append(jnp.where(lo_q, qp, zero_q))
                q_rows.append(jnp.where(lo_q, zero_q, qp))
            q_stack = jnp.concatenate(q_rows, axis=0)
            s_all = lax.dot_general(q_stack, k_dup, (((1,), (1,)), ((), ())),
                                    preferred_element_type=jnp.float32)
            p_rows, inv_rows = [], []
            for g in range(group):
                sink = sink_ref[hk * group + g]
                sc = jnp.where(valid, s_all[g * bq:(g + 1) * bq, :], -jnp.inf)
                m = jnp.maximum(jnp.max(sc, axis=-1, keepdims=True), sink)
                e = jnp.exp(sc - m)
                denom = jnp.sum(e, axis=-1, keepdims=True) + jnp.exp(sink - m)
                p_rows.append(e.astype(v_dup.dtype))
                inv_rows.append(1.0 / denom)
            p_stack = jnp.concatenate(p_rows, axis=0)
            o_all = jnp.dot(p_stack, v_dup, preferred_element_type=jnp.float32)
            for p in range(pairs_per_group):
                pair = hk * pairs_per_group + p
                o_even = o_all[(2 * p) * bq:(2 * p + 1) * bq, :] * inv_rows[2 * p]
                o_odd = o_all[(2 * p + 1) * bq:(2 * p + 2) * bq, :] * inv_rows[2 * p + 1]
                o_pair = jnp.where(lo_q, o_even, o_odd)
                c0 = pair * pair_w
                o_ref[:, c0:c0 + pair_w] = (o_pair * sg_ref[:, c0:c0 + pair_w]).astype(o_ref.dtype)


def _attn_kernel(sink_ref, q_ref, kv_ref, kvp_ref, sg_ref, cast_src, o_ref, cast_dst, *,
                 n_kv, group, head_dim):
    _run_cast(cast_src, cast_dst)
    bq = BLOCK_Q
    n = pl.program_id(0)
    for b in range(q_ref.shape[0] // bq):
        rows = pl.ds(b * bq, bq)
        k_prev = kvp_ref if b == 0 else kv_ref.at[pl.ds((b - 1) * bq, bq), :]
        first = (n == 0) if b == 0 else jnp.bool_(False)
        _attn_block(first, sink_ref, q_ref.at[rows, :], k_prev, kv_ref.at[rows, :],
                    sg_ref.at[rows, :], o_ref.at[rows, :], n_kv=n_kv, group=group,
                    head_dim=head_dim)


def _attention(q, kv, sg_attn, sinks, cast, *, n_kv, head_dim):
    s, d_attn = q.shape
    n_q = d_attn // head_dim
    group = n_q // n_kv
    assert head_dim * 2 == LANES and group % 2 == 0 and n_kv % 2 == 0
    bq = BLOCK_Q
    rows = ATTN_BLOCKS_PER_STEP * bq
    d_kv2 = kv.shape[1]
    grid = (s // rows, 1)
    c_in, c_out, c_shape, c_bytes = _cast_specs(cast, grid)
    block_bytes = 2 * rows * (d_attn * (2 + 4 + 2) + d_kv2 * 2) + 2 * bq * d_kv2 * 2 + c_bytes
    seq = pl.BlockSpec((rows, d_attn), lambda n, _: (n, 0))
    return pl.pallas_call(
        functools.partial(_attn_kernel, n_kv=n_kv, group=group, head_dim=head_dim),
        out_shape=(jax.ShapeDtypeStruct((s, d_attn), jnp.bfloat16), c_shape),
        grid=grid,
        in_specs=[pl.BlockSpec(memory_space=pltpu.SMEM),
                  seq,
                  pl.BlockSpec((rows, d_kv2), lambda n, _: (n, 0)),
                  pl.BlockSpec((bq, d_kv2),
                               lambda n, _: (jnp.maximum(n * ATTN_BLOCKS_PER_STEP - 1, 0), 0)),
                  seq, c_in],
        out_specs=(seq, c_out),
        compiler_params=pltpu.CompilerParams(
            dimension_semantics=("arbitrary", "arbitrary"),
            vmem_limit_bytes=_vmem_limit(block_bytes + 16 * 1024 * 1024)),
        name="swa_attn",
    )(sinks, q, kv, kv, sg_attn, cast.src)


def kernel(x, c, w_ada, b_ada, norm_gain, w_in, conv_w, conv_b, w_gate_a, b_gate_a, w_gate_x,
           b_gate_x, lru_lambda, attn_sinks, w_rnn_proj, w_attn_proj, w_out, final_norm_gain):
    bsz, seq, d = x.shape
    depth = w_in.shape[0]
    d_rnn = w_rnn_proj.shape[1]
    d_attn = w_attn_proj.shape[1]
    n_q = attn_sinks.shape[1]
    head_dim = d_attn // n_q
    d_kv = (w_in.shape[2] - 2 * d_rnn - 2 * d_attn - 2 * d) // 2
    n_kv = d_kv // head_dim
    assert bsz == 1

    xs = x.reshape(bsz * seq, d)
    bf = jnp.bfloat16
    for l in range(depth):
        mod = _modulation(c, w_ada[l], b_ada[l])
        shift, scale, gate = mod[:, :d], mod[:, d:2 * d], mod[:, 2 * d:]
        h = _norm_mod(xs, norm_gain[l].reshape(1, d), scale, shift)

        wl = w_in[l]
        c_u, c_g, c_q, c_kv = 0, d_rnn, 2 * d_rnn, 2 * d_rnn + d_attn
        c_ga = c_kv + 2 * d_kv
        c_mr, c_ma = c_ga + d_attn, c_ga + d_attn + d
        proj = functools.partial(_matmul, h)
        w_kv = _cast_cols(CastJob(wl, c_kv, 2 * d_kv))
        kv, w_u = proj(w_kv, out_dtype=bf, cast=CastJob(wl, c_u, d_rnn), name="proj_kv")
        u_rnn, w_g = proj(w_u, out_dtype=jnp.float32, cast=CastJob(wl, c_g, d_rnn),
                          name="proj_u")
        sg_rnn, w_q = proj(w_g, out_dtype=jnp.float32, epilogue=_ep_silu,
                           cast=CastJob(wl, c_q, d_attn), name="proj_g_rnn")
        q, w_ga = proj(w_q, out_dtype=bf, cast=CastJob(wl, c_ga, d_attn), name="proj_q")
        sg_attn, w_mr = proj(w_ga, out_dtype=jnp.float32, epilogue=_ep_silu,
                             cast=CastJob(wl, c_mr, d), name="proj_g_attn")
        sm_rnn, w_ma = proj(w_mr, out_dtype=jnp.float32, epilogue=_ep_sigmoid,
                            cast=CastJob(wl, c_ma, d), name="proj_m_rnn")
        sm_attn, w_rp = proj(w_ma, out_dtype=jnp.float32, epilogue=_ep_sigmoid,
                             cast=CastJob(w_rnn_proj[l], 0, d), name="proj_m_attn")

        y_rnn, w_ap = _rglru(u_rnn, sg_rnn, conv_w[l], conv_b[l], w_gate_a[l], b_gate_a[l],
                             w_gate_x[l], b_gate_x[l], lru_lambda[l],
                             CastJob(w_attn_proj[l], 0, d))
        y_attn, w_o = _attention(q, kv, sg_attn, attn_sinks[l], CastJob(w_out[l], 0, d),
                                 n_kv=n_kv, head_dim=head_dim)

        merged = _merge(y_rnn, y_attn, w_rp, w_ap, sm_rnn, sm_attn)
        if l + 1 < depth:
            xs = _matmul(merged, w_o, out_dtype=jnp.float32, epilogue=_ep_residual,
                         tiles=(xs,), rows=(gate,), name="out_proj")
        else:
            xs = _out_norm(merged, w_o, xs, gate, final_norm_gain.reshape(1, d))
    return xs.reshape(bsz, seq, d)
```

```python
import functools
from typing import NamedTuple

import jax
import jax.numpy as jnp
from jax import lax
from jax.experimental import pallas as pl
from jax.experimental.pallas import tpu as pltpu

EPS = 1e-6
LRU_C = 8.0
WINDOW = 128
BLOCK_Q = 128
ATTN_BLOCKS_PER_STEP = 2

V7X_VMEM_BYTES = 64 * 1024 * 1024
VMEM_HEADROOM_BYTES = 6 * 1024 * 1024
LANES = 128
SUBLANES = 8


def _vmem_limit(block_bytes):
    need = int(block_bytes) + VMEM_HEADROOM_BYTES
    assert need <= V7X_VMEM_BYTES - 2 * 1024 * 1024, need
    return need


def _sigmoid(v):
    return 0.5 * jnp.tanh(0.5 * v) + 0.5


def _silu(v):
    return v * _sigmoid(v)


def _mod_kernel(c_ref, w_ref, b_ref, o_ref):
    c_act = _silu(c_ref[...])
    o_ref[...] = jnp.sum(w_ref[...] * c_act, axis=0, keepdims=True) + b_ref[...]


def _modulation(c, w_ada, b_ada):
    d, n = w_ada.shape
    tn = 512
    return pl.pallas_call(
        _mod_kernel,
        out_shape=jax.ShapeDtypeStruct((1, n), jnp.float32),
        grid=(n // tn,),
        in_specs=[pl.BlockSpec((d, 1), lambda j: (0, 0)),
                  pl.BlockSpec((d, tn), lambda j: (0, j)),
                  pl.BlockSpec((1, tn), lambda j: (0, j))],
        out_specs=pl.BlockSpec((1, tn), lambda j: (0, j)),
        compiler_params=pltpu.CompilerParams(
            dimension_semantics=("arbitrary",),
            vmem_limit_bytes=_vmem_limit(2 * d * tn * 4 + 2 * d * LANES * 4 + d * tn * 4)),
        name="adaln_mod",
    )(c.reshape(d, 1), w_ada, b_ada.reshape(1, n))


def _norm_mod_kernel(x_ref, gain_ref, scale_ref, shift_ref, o_ref):
    x = x_ref[...]
    ms = jnp.mean(x * x, axis=-1, keepdims=True)
    g = gain_ref[...] * (1.0 + scale_ref[...])
    o_ref[...] = (x * lax.rsqrt(ms + EPS) * g + shift_ref[...]).astype(o_ref.dtype)


def _norm_mod(x2, gain, scale, shift):
    s, d = x2.shape
    tm = 512
    row = pl.BlockSpec((1, d), lambda i: (0, 0))
    return pl.pallas_call(
        _norm_mod_kernel,
        out_shape=jax.ShapeDtypeStruct((s, d), jnp.bfloat16),
        grid=(s // tm,),
        in_specs=[pl.BlockSpec((tm, d), lambda i: (i, 0)), row, row, row],
        out_specs=pl.BlockSpec((tm, d), lambda i: (i, 0)),
        compiler_params=pltpu.CompilerParams(
            dimension_semantics=("arbitrary",),
            vmem_limit_bytes=_vmem_limit(2 * tm * d * (4 + 2) + 2 * tm * d * 4)),
        name="norm_mod",
    )(x2, gain, scale, shift)


CAST_BLOCK_COLS = 1024


class CastJob(NamedTuple):
    src: jax.Array
    col0: int
    cols: int


def _cast_specs(job, grid):
    g0, g1 = grid
    n_steps = g0 * g1
    rows = job.src.shape[0]
    cw = min(job.cols, CAST_BLOCK_COLS)
    n_cb = job.cols // cw
    rb = rows * n_cb // n_steps
    assert rb * n_steps == rows * n_cb and rb % 16 == 0 and job.col0 % cw == 0
    col_blk0 = job.col0 // cw

    def blk(i, j):
        s = i * g1 + j
        return s // n_cb, s % n_cb

    in_spec = pl.BlockSpec((rb, cw), lambda i, j: (blk(i, j)[0], col_blk0 + blk(i, j)[1]))
    out_spec = pl.BlockSpec((rb, cw), lambda i, j: blk(i, j))
    out_shape = jax.ShapeDtypeStruct((rows, job.cols), jnp.bfloat16)
    return in_spec, out_spec, out_shape, 2 * rb * cw * (4 + 2)


def _run_cast(src_ref, dst_ref):
    dst_ref[...] = src_ref[...].astype(dst_ref.dtype)


def _cast_cols(job):
    rows = job.src.shape[0]
    rb = 512
    cw = min(job.cols, CAST_BLOCK_COLS)
    assert rows % rb == 0 and job.cols % cw == 0 and job.col0 % cw == 0
    col_blk0 = job.col0 // cw
    return pl.pallas_call(
        _run_cast,
        out_shape=jax.ShapeDtypeStruct((rows, job.cols), jnp.bfloat16),
        grid=(rows // rb, job.cols // cw),
        in_specs=[pl.BlockSpec((rb, cw), lambda i, j: (i, col_blk0 + j))],
        out_specs=pl.BlockSpec((rb, cw), lambda i, j: (i, j)),
        compiler_params=pltpu.CompilerParams(
            dimension_semantics=("arbitrary", "arbitrary"),
            vmem_limit_bytes=_vmem_limit(2 * rb * cw * (4 + 2))),
        name="cast_cols",
    )(job.src)


def _ep_identity(acc):
    return acc


def _ep_silu(acc):
    return _silu(acc)


def _ep_sigmoid(acc):
    return _sigmoid(acc)


def _ep_residual(acc, x, gate):
    return x + gate * acc


def _matmul_kernel(*refs, epilogue, n_tiles, n_rows, has_cast):
    lhs_ref, rhs_ref = refs[0], refs[1]
    n_in = 2 + n_tiles + n_rows
    tile_refs = refs[2:2 + n_tiles]
    row_refs = refs[2 + n_tiles:n_in]
    o_ref = refs[n_in + has_cast]
    if has_cast:
        _run_cast(refs[n_in], refs[n_in + 2])
    acc = jnp.dot(lhs_ref[...], rhs_ref[...], preferred_element_type=jnp.float32)
    extras = [r[...] for r in tile_refs] + [r[...] for r in row_refs]
    o_ref[...] = epilogue(acc, *extras).astype(o_ref.dtype)


def _matmul(lhs, rhs, *, out_dtype, epilogue=_ep_identity, tiles=(), rows=(), tn=1024,
            cast=None, name):
    m, k = lhs.shape
    n_cols = rhs.shape[1]
    tm = 512 if tiles else 1024
    assert rhs.shape[0] == k and n_cols % tn == 0 and m % tm == 0
    grid = (m // tm, n_cols // tn)
    in_specs = [pl.BlockSpec((tm, k), lambda i, j: (i, 0)),
                pl.BlockSpec((k, tn), lambda i, j: (0, j))]
    in_specs += [pl.BlockSpec((tm, tn), lambda i, j: (i, j)) for _ in tiles]
    in_specs += [pl.BlockSpec((1, tn), lambda i, j: (0, j)) for _ in rows]
    out_bytes = jnp.dtype(out_dtype).itemsize
    tile_bytes = sum(t.dtype.itemsize for t in tiles)
    block_bytes = 2 * (tm * k * lhs.dtype.itemsize + k * tn * rhs.dtype.itemsize
                       + tm * tn * (out_bytes + tile_bytes)) + 2 * tm * tn * 4
    out_shape = jax.ShapeDtypeStruct((m, n_cols), out_dtype)
    out_spec = pl.BlockSpec((tm, tn), lambda i, j: (i, j))
    args = (lhs, rhs, *tiles, *rows)
    if cast is not None:
        c_in, c_out, c_shape, c_bytes = _cast_specs(cast, grid)
        in_specs.append(c_in)
        out_shape, out_spec = (out_shape, c_shape), (out_spec, c_out)
        args += (cast.src,)
        block_bytes += c_bytes
    return pl.pallas_call(
        functools.partial(_matmul_kernel, epilogue=epilogue, n_tiles=len(tiles),
                          n_rows=len(rows), has_cast=cast is not None),
        out_shape=out_shape,
        grid=grid,
        in_specs=in_specs,
        out_specs=out_spec,
        compiler_params=pltpu.CompilerParams(
            dimension_semantics=("arbitrary", "arbitrary"),
            vmem_limit_bytes=_vmem_limit(block_bytes)),
        name=name,
    )(*args)


def _merge_kernel(yr_ref, ya_ref, wr_ref, wa_ref, smr_ref, sma_ref, o_ref):
    p_rnn = jnp.dot(yr_ref[...], wr_ref[...], preferred_element_type=jnp.float32)
    p_attn = jnp.dot(ya_ref[...], wa_ref[...], preferred_element_type=jnp.float32)
    o_ref[...] = (smr_ref[...] * p_rnn + sma_ref[...] * p_attn).astype(o_ref.dtype)


def _merge(y_rnn, y_attn, w_rnn, w_attn, sm_rnn, sm_attn):
    m, k = y_rnn.shape
    n = w_rnn.shape[1]
    tm, tn = 1024, 512
    lhs = pl.BlockSpec((tm, k), lambda j, i: (i, 0))
    rhs = pl.BlockSpec((k, tn), lambda j, i: (0, j), pipeline_mode=pl.Buffered(1))
    tile = pl.BlockSpec((tm, tn), lambda j, i: (i, j))
    block_bytes = (2 * (2 * tm * k * 2 + tm * tn * (4 + 4 + 2)) + 2 * k * tn * 2
                   + 2 * tm * tn * 4)
    return pl.pallas_call(
        _merge_kernel,
        out_shape=jax.ShapeDtypeStruct((m, n), jnp.bfloat16),
        grid=(n // tn, m // tm),
        in_specs=[lhs, lhs, rhs, rhs, tile, tile],
        out_specs=tile,
        compiler_params=pltpu.CompilerParams(
            dimension_semantics=("arbitrary", "arbitrary"),
            vmem_limit_bytes=_vmem_limit(block_bytes)),
        name="merge_proj",
    )(y_rnn, y_attn, w_rnn, w_attn, sm_rnn, sm_attn)


def _out_norm_kernel(m_ref, w_ref, x_ref, gate_ref, gain_ref, o_ref, res_ref):
    j = pl.program_id(1)
    n_j = pl.num_programs(1)
    tn = w_ref.shape[1]
    acc = jnp.dot(m_ref[...], w_ref[...], preferred_element_type=jnp.float32)
    res_ref[j] = x_ref[...] + gate_ref[...] * acc

    @pl.when(j == n_j - 1)
    def _():
        n_tiles = res_ref.shape[0]
        ss = jnp.sum(res_ref[0] * res_ref[0], axis=-1, keepdims=True)
        for t in range(1, n_tiles):
            ss = ss + jnp.sum(res_ref[t] * res_ref[t], axis=-1, keepdims=True)
        inv = lax.rsqrt(ss / (n_tiles * tn) + EPS)
        for t in range(n_tiles):
            o_ref[:, t * tn:(t + 1) * tn] = res_ref[t] * inv * gain_ref[:, t * tn:(t + 1) * tn]


def _out_norm(merged, w, x2, gate, gain):
    m, k = merged.shape
    n = w.shape[1]
    tm, tn = 512, 1024
    block_bytes = (2 * (tm * k * 2 + k * tn * 2 + tm * tn * 4 + tm * n * 4) + tm * n * 4
                   + tm * tn * 4)
    return pl.pallas_call(
        _out_norm_kernel,
        out_shape=jax.ShapeDtypeStruct((m, n), jnp.float32),
        grid=(m // tm, n // tn),
        in_specs=[pl.BlockSpec((tm, k), lambda i, j: (i, 0)),
                  pl.BlockSpec((k, tn), lambda i, j: (0, j)),
                  pl.BlockSpec((tm, tn), lambda i, j: (i, j)),
                  pl.BlockSpec((1, tn), lambda i, j: (0, j)),
                  pl.BlockSpec((1, n), lambda i, j: (0, 0))],
        out_specs=pl.BlockSpec((tm, n), lambda i, j: (i, 0)),
        scratch_shapes=[pltpu.VMEM((n // tn, tm, tn), jnp.float32)],
        compiler_params=pltpu.CompilerParams(
            dimension_semantics=("arbitrary", "arbitrary"),
            vmem_limit_bytes=_vmem_limit(block_bytes)),
        name="out_norm",
    )(merged, w, x2, gate, gain)


RNN_SUB_ROWS = 512
RNN_BLOCK_ROWS = 4096
RNN_HALO_GROUPS = 4
SQRT_FLOOR = 1e-30


def _slabs(ref, rows, n_slabs):
    return jnp.concatenate([ref[l, rows, :] for l in range(n_slabs)], axis=1)


def _rglru_kernel(u_ref, sg_ref, cw_ref, cb_ref, wa_ref, ba_ref, wx_ref, bx_ref, lam_ref,
                  cast_src, y_ref, cast_dst, uperm, hbuf, pbuf, carry_ref, *, conv_width):
    ts = RNN_SUB_ROWS
    tc = ts // SUBLANES
    width = u_ref.shape[1]
    n_slabs = width // LANES
    data0 = RNN_HALO_GROUPS * SUBLANES
    taps = conv_width - 1
    assert taps < RNN_HALO_GROUPS and taps <= tc
    _run_cast(cast_src, cast_dst)

    @pl.when(pl.program_id(1) == 0)
    def _():
        uperm[...] = jnp.zeros_like(uperm)
        carry_ref[...] = jnp.zeros_like(carry_ref)

    neg = -lam_ref[...]
    softplus_neg = jnp.maximum(neg, 0.0) + jnp.log1p(jnp.exp(-jnp.abs(neg)))
    half_decay = (-0.5 * LRU_C) * softplus_neg
    half_ba = 0.5 * ba_ref[...]
    half_bx = 0.5 * bx_ref[...]
    cw = cw_ref[...]
    sub = lax.broadcasted_iota(jnp.int32, (SUBLANES, width), 0)

    def group(j):
        return pl.ds(data0 + j * SUBLANES, SUBLANES)

    def sub_tile(s, _):
        r0 = pl.multiple_of(s * ts, ts)
        prev = [_slabs(uperm, group(tc - k), n_slabs) for k in range(1, taps + 1)]
        for c in range(SUBLANES):
            for l in range(n_slabs):
                uperm[l, pl.ds(data0 + c, tc, stride=SUBLANES), :] = (
                    u_ref[pl.ds(r0 + c * tc, tc), l * LANES:(l + 1) * LANES])
        for k in range(1, taps + 1):
            cur = _slabs(uperm, group(tc - k), n_slabs)
            halo = jnp.where(sub == 0, pltpu.roll(prev[k - 1], 1, axis=0),
                             pltpu.roll(cur, 1, axis=0))
            for l in range(n_slabs):
                uperm[l, group(-k), :] = halo[:, l * LANES:(l + 1) * LANES]
        u = cb_ref[...]
        for k in range(conv_width):
            first = data0 - (taps - k) * SUBLANES
            u = u + cw[k:k + 1, :] * _slabs(uperm, pl.ds(first, ts), n_slabs)
        ub = u.astype(jnp.bfloat16)
        t_r = jnp.tanh(jnp.dot(ub, wa_ref[...], preferred_element_type=jnp.float32) + half_ba)
        t_i = jnp.tanh(jnp.dot(ub, wx_ref[...], preferred_element_type=jnp.float32) + half_bx)
        log_a = half_decay * t_r + half_decay
        a = jnp.exp(log_a)
        half_u = 0.5 * u
        gated_u = half_u * t_i + half_u
        one_m_a2 = 1.0 - a * a
        b = (one_m_a2 * lax.rsqrt(jnp.maximum(one_m_a2, SQRT_FLOOR))) * gated_u

        h_loc = b[0:SUBLANES]
        p_cum = a[0:SUBLANES]
        for j in range(tc):
            if j > 0:
                aj = a[j * SUBLANES:(j + 1) * SUBLANES]
                h_loc = aj * h_loc + b[j * SUBLANES:(j + 1) * SUBLANES]
                p_cum = aj * p_cum
            pbuf[j * SUBLANES:(j + 1) * SUBLANES, :] = p_cum
            for l in range(n_slabs):
                hbuf[l, j * SUBLANES:(j + 1) * SUBLANES, :] = h_loc[:, l * LANES:(l + 1) * LANES]

        a_inc, b_inc = p_cum, h_loc
        shift = 1
        while shift < SUBLANES:
            keep = sub >= shift
            a_prev = jnp.where(keep, pltpu.roll(a_inc, shift, axis=0), 1.0)
            b_prev = jnp.where(keep, pltpu.roll(b_inc, shift, axis=0), 0.0)
            b_inc = a_inc * b_prev + b_inc
            a_inc = a_inc * a_prev
            shift *= 2
        h_in = jnp.broadcast_to(carry_ref[...], (SUBLANES, width))
        chunk_end = b_inc + a_inc * h_in
        entry = jnp.where(sub == 0, h_in, pltpu.roll(chunk_end, 1, axis=0))
        carry_ref[...] = chunk_end[SUBLANES - 1:SUBLANES, :]

        for j in range(tc):
            rows = slice(j * SUBLANES, (j + 1) * SUBLANES)
            h = _slabs(hbuf, rows, n_slabs) + pbuf[rows, :] * entry
            for l in range(n_slabs):
                hbuf[l, rows, :] = h[:, l * LANES:(l + 1) * LANES]
        for c in range(SUBLANES):
            rows = pl.ds(r0 + c * tc, tc)
            h_nat = _slabs(hbuf, pl.ds(c, tc, stride=SUBLANES), n_slabs)
            y_ref[rows, :] = (h_nat * sg_ref[rows, :]).astype(y_ref.dtype)
        return 0

    lax.fori_loop(0, u_ref.shape[0] // ts, sub_tile, 0, unroll=2)


def _rglru(u_rnn, sg_rnn, conv_w, conv_b, w_gate_a, b_gate_a, w_gate_x, b_gate_x, lam, cast):
    s, d = u_rnn.shape
    nb, bw, _ = w_gate_a.shape
    conv_width = conv_w.shape[0]
    tb = RNN_BLOCK_ROWS
    seq = pl.BlockSpec((tb, bw), lambda c, t: (t, c))
    vec = pl.BlockSpec((1, bw), lambda c, t: (0, c))
    gate_w = pl.BlockSpec((None, bw, bw), lambda c, t: (c, 0, 0))
    gate_b = pl.BlockSpec((None, 1, bw), lambda c, t: (c, 0, 0))
    block_bytes = (2 * tb * bw * (4 + 4 + 2) + 4 * bw * bw * 2
                   + 3 * (RNN_SUB_ROWS + RNN_HALO_GROUPS * SUBLANES) * bw * 4)
    grid = (nb, s // tb)
    c_in, c_out, c_shape, c_bytes = _cast_specs(cast, grid)
    return pl.pallas_call(
        functools.partial(_rglru_kernel, conv_width=conv_width),
        out_shape=(jax.ShapeDtypeStruct((s, d), jnp.bfloat16), c_shape),
        grid=grid,
        in_specs=[seq, seq,
                  pl.BlockSpec((conv_width, bw), lambda c, t: (0, c)), vec,
                  gate_w, gate_b, gate_w, gate_b, vec, c_in],
        out_specs=(seq, c_out),
        scratch_shapes=[
            pltpu.VMEM((bw // LANES, RNN_SUB_ROWS + RNN_HALO_GROUPS * SUBLANES, LANES),
                       jnp.float32),
            pltpu.VMEM((bw // LANES, RNN_SUB_ROWS, LANES), jnp.float32),
            pltpu.VMEM((RNN_SUB_ROWS, bw), jnp.float32),
            pltpu.VMEM((1, bw), jnp.float32)],
        compiler_params=pltpu.CompilerParams(
            dimension_semantics=("arbitrary", "arbitrary"),
            vmem_limit_bytes=_vmem_limit(block_bytes + c_bytes + 16 * 1024 * 1024)),
        name="rglru",
    )(u_rnn, sg_rnn, conv_w, conv_b.reshape(1, d),
      (0.5 * w_gate_a).astype(jnp.bfloat16), b_gate_a.reshape(nb, 1, bw),
      (0.5 * w_gate_x).astype(jnp.bfloat16), b_gate_x.reshape(nb, 1, bw), lam.reshape(1, d),
      cast.src)


def _swap_halves(v):
    half = v.shape[1] // 2
    return jnp.concatenate([v[:, half:], v[:, :half]], axis=1)


def _attn_block(first, sink_ref, q_ref, k_prev, k_cur, sg_ref, o_ref, *, n_kv, group, head_dim):
    bq = q_ref.shape[0]
    pair_w = 2 * head_dim
    pairs_per_group = group // 2
    d_kv = n_kv * head_dim

    q_pos = lax.broadcasted_iota(jnp.int32, (bq, 2 * bq), 0)
    k_pos = lax.broadcasted_iota(jnp.int32, (bq, 2 * bq), 1)
    dist = q_pos + bq - k_pos
    valid = (dist >= 0) & (dist < WINDOW) & ((k_pos >= bq) | jnp.logical_not(first))
    lane = lax.broadcasted_iota(jnp.int32, (bq, pair_w), 1)
    lo_q = lane < head_dim
    lane_kv = lax.broadcasted_iota(jnp.int32, (2 * bq, pair_w), 1)
    lo_kv = lane_kv < head_dim
    scale = head_dim ** -0.5
    zero_q = jnp.zeros((bq, pair_w), q_ref.dtype)

    for kv_pair in range(n_kv // 2):
        cols = slice(kv_pair * pair_w, (kv_pair + 1) * pair_w)
        k_two = jnp.concatenate([k_prev[:, cols], k_cur[:, cols]], axis=0)
        v_cols = slice(d_kv + kv_pair * pair_w, d_kv + (kv_pair + 1) * pair_w)
        v_two = jnp.concatenate([k_prev[:, v_cols], k_cur[:, v_cols]], axis=0)
        k_swapped = _swap_halves(k_two)
        v_swapped = _swap_halves(v_two)
        for parity in range(2):
            hk = 2 * kv_pair + parity
            if parity == 0:
                k_dup = jnp.where(lo_kv, k_two, k_swapped)
                v_dup = jnp.where(lo_kv, v_two, v_swapped)
            else:
                k_dup = jnp.where(lo_kv, k_swapped, k_two)
                v_dup = jnp.where(lo_kv, v_swapped, v_two)
            k_dup = k_dup * jnp.asarray(scale, k_dup.dtype)
            q_rows = []
            for p in range(pairs_per_group):
                pair = hk * pairs_per_group + p
                qp = q_ref[:, pair * pair_w:(pair + 1) * pair_w]
                q_rows.append(jnp.where(lo_q, qp, zero_q))
                q_rows.append(jnp.where(lo_q, zero_q, qp))
            q_stack = jnp.concatenate(q_rows, axis=0)
            s_all = lax.dot_general(q_stack, k_dup, (((1,), (1,)), ((), ())),
                                    preferred_element_type=jnp.float32)
            p_rows, inv_rows = [], []
            for g in range(group):
                sink = sink_ref[hk * group + g]
                sc = jnp.where(valid, s_all[g * bq:(g + 1) * bq, :], -jnp.inf)
                m = jnp.maximum(jnp.max(sc, axis=-1, keepdims=True), sink)
                e = jnp.exp(sc - m)
                denom = jnp.sum(e, axis=-1, keepdims=True) + jnp.exp(sink - m)
                p_rows.append(e.astype(v_dup.dtype))
                inv_rows.append(1.0 / denom)
            p_stack = jnp.concatenate(p_rows, axis=0)
            o_all = jnp.dot(p_stack, v_dup, preferred_element_type=jnp.float32)
            for p in range(pairs_per_group):
                pair = hk * pairs_per_group + p
                o_even = o_all[(2 * p) * bq:(2 * p + 1) * bq, :] * inv_rows[2 * p]
                o_odd = o_all[(2 * p + 1) * bq:(2 * p + 2) * bq, :] * inv_rows[2 * p + 1]
                o_pair = jnp.where(lo_q, o_even, o_odd)
                c0 = pair * pair_w
                o_ref[:, c0:c0 + pair_w] = (o_pair * sg_ref[:, c0:c0 + pair_w]).astype(o_ref.dtype)


def _attn_kernel(sink_ref, q_ref, kv_ref, kvp_ref, sg_ref, cast_src, o_ref, cast_dst, *,
                 n_kv, group, head_dim):
    _run_cast(cast_src, cast_dst)
    bq = BLOCK_Q
    n = pl.program_id(0)
    for b in range(q_ref.shape[0] // bq):
        rows = pl.ds(b * bq, bq)
        k_prev = kvp_ref if b == 0 else kv_ref.at[pl.ds((b - 1) * bq, bq), :]
        first = (n == 0) if b == 0 else jnp.bool_(False)
        _attn_block(first, sink_ref, q_ref.at[rows, :], k_prev, kv_ref.at[rows, :],
                    sg_ref.at[rows, :], o_ref.at[rows, :], n_kv=n_kv, group=group,
                    head_dim=head_dim)


def _attention(q, kv, sg_attn, sinks, cast, *, n_kv, head_dim):
    s, d_attn = q.shape
    n_q = d_attn // head_dim
    group = n_q // n_kv
    assert head_dim * 2 == LANES and group % 2 == 0 and n_kv % 2 == 0
    bq = BLOCK_Q
    rows = ATTN_BLOCKS_PER_STEP * bq
    d_kv2 = kv.shape[1]
    grid = (s // rows, 1)
    c_in, c_out, c_shape, c_bytes = _cast_specs(cast, grid)
    block_bytes = 2 * rows * (d_attn * (2 + 4 + 2) + d_kv2 * 2) + 2 * bq * d_kv2 * 2 + c_bytes
    seq = pl.BlockSpec((rows, d_attn), lambda n, _: (n, 0))
    return pl.pallas_call(
        functools.partial(_attn_kernel, n_kv=n_kv, group=group, head_dim=head_dim),
        out_shape=(jax.ShapeDtypeStruct((s, d_attn), jnp.bfloat16), c_shape),
        grid=grid,
        in_specs=[pl.BlockSpec(memory_space=pltpu.SMEM),
                  seq,
                  pl.BlockSpec((rows, d_kv2), lambda n, _: (n, 0)),
                  pl.BlockSpec((bq, d_kv2),
                               lambda n, _: (jnp.maximum(n * ATTN_BLOCKS_PER_STEP - 1, 0), 0)),
                  seq, c_in],
        out_specs=(seq, c_out),
        compiler_params=pltpu.CompilerParams(
            dimension_semantics=("arbitrary", "arbitrary"),
            vmem_limit_bytes=_vmem_limit(block_bytes + 16 * 1024 * 1024)),
        name="swa_attn",
    )(sinks, q, kv, kv, sg_attn, cast.src)


def kernel(x, c, w_ada, b_ada, norm_gain, w_in, conv_w, conv_b, w_gate_a, b_gate_a, w_gate_x,
           b_gate_x, lru_lambda, attn_sinks, w_rnn_proj, w_attn_proj, w_out, final_norm_gain):
    bsz, seq, d = x.shape
    depth = w_in.shape[0]
    d_rnn = w_rnn_proj.shape[1]
    d_attn = w_attn_proj.shape[1]
    n_q = attn_sinks.shape[1]
    head_dim = d_attn // n_q
    d_kv = (w_in.shape[2] - 2 * d_rnn - 2 * d_attn - 2 * d) // 2
    n_kv = d_kv // head_dim
    assert bsz == 1

    xs = x.reshape(bsz * seq, d)
    bf = jnp.bfloat16
    for l in range(depth):
        mod = _modulation(c, w_ada[l], b_ada[l])
        shift, scale, gate = mod[:, :d], mod[:, d:2 * d], mod[:, 2 * d:]
        h = _norm_mod(xs, norm_gain[l].reshape(1, d), scale, shift)

        wl = w_in[l]
        c_u, c_g, c_q, c_kv = 0, d_rnn, 2 * d_rnn, 2 * d_rnn + d_attn
        c_ga = c_kv + 2 * d_kv
        c_mr, c_ma = c_ga + d_attn, c_ga + d_attn + d
        proj = functools.partial(_matmul, h)
        w_kv = _cast_cols(CastJob(wl, c_kv, 2 * d_kv))
        kv, w_u = proj(w_kv, out_dtype=bf, cast=CastJob(wl, c_u, d_rnn), name="proj_kv")
        u_rnn, w_g = proj(w_u, out_dtype=jnp.float32, cast=CastJob(wl, c_g, d_rnn),
                          name="proj_u")
        sg_rnn, w_q = proj(w_g, out_dtype=jnp.float32, epilogue=_ep_silu,
                           cast=CastJob(wl, c_q, d_attn), name="proj_g_rnn")
        q, w_ga = proj(w_q, out_dtype=bf, cast=CastJob(wl, c_ga, d_attn), name="proj_q")
        sg_attn, w_mr = proj(w_ga, out_dtype=jnp.float32, epilogue=_ep_silu,
                             cast=CastJob(wl, c_mr, d), name="proj_g_attn")
        sm_rnn, w_ma = proj(w_mr, out_dtype=jnp.float32, epilogue=_ep_sigmoid,
                            cast=CastJob(wl, c_ma, d), name="proj_m_rnn")
        sm_attn, w_rp = proj(w_ma, out_dtype=jnp.float32, epilogue=_ep_sigmoid,
                             cast=CastJob(w_rnn_proj[l], 0, d), name="proj_m_attn")

        y_rnn, w_ap = _rglru(u_rnn, sg_rnn, conv_w[l], conv_b[l], w_gate_a[l], b_gate_a[l],
                             w_gate_x[l], b_gate_x[l], lru_lambda[l],
                             CastJob(w_attn_proj[l], 0, d))
        y_attn, w_o = _attention(q, kv, sg_attn, attn_sinks[l], CastJob(w_out[l], 0, d),
                                 n_kv=n_kv, head_dim=head_dim)

        merged = _merge(y_rnn, y_attn, w_rp, w_ap, sm_rnn, sm_attn)
        if l + 1 < depth:
            xs = _matmul(merged, w_o, out_dtype=jnp.float32, epilogue=_ep_residual,
                         tiles=(xs,), rows=(gate,), name="out_proj")
        else:
            xs = _out_norm(merged, w_o, xs, gate, final_norm_gain.reshape(1, d))
    return xs.reshape(bsz, seq, d)
```

```python
import functools
from typing import NamedTuple

import jax
import jax.numpy as jnp
from jax import lax
from jax.experimental import pallas as pl
from jax.experimental.pallas import tpu as pltpu

EPS = 1e-6
LRU_C = 8.0
WINDOW = 128
BLOCK_Q = 128
ATTN_BLOCKS_PER_STEP = 2

V7X_VMEM_BYTES = 64 * 1024 * 1024
VMEM_HEADROOM_BYTES = 6 * 1024 * 1024
LANES = 128
SUBLANES = 8


def _vmem_limit(block_bytes):
    need = int(block_bytes) + VMEM_HEADROOM_BYTES
    assert need <= V7X_VMEM_BYTES - 2 * 1024 * 1024, need
    return need


def _sigmoid(v):
    return 0.5 * jnp.tanh(0.5 * v) + 0.5


def _silu(v):
    return v * _sigmoid(v)


def _mod_kernel(c_ref, w_ref, b_ref, o_ref):
    c_act = _silu(c_ref[...])
    o_ref[...] = jnp.sum(w_ref[...] * c_act, axis=0, keepdims=True) + b_ref[...]


def _modulation(c, w_ada, b_ada):
    d, n = w_ada.shape
    tn = 512
    return pl.pallas_call(
        _mod_kernel,
        out_shape=jax.ShapeDtypeStruct((1, n), jnp.float32),
        grid=(n // tn,),
        in_specs=[pl.BlockSpec((d, 1), lambda j: (0, 0)),
                  pl.BlockSpec((d, tn), lambda j: (0, j)),
                  pl.BlockSpec((1, tn), lambda j: (0, j))],
        out_specs=pl.BlockSpec((1, tn), lambda j: (0, j)),
        compiler_params=pltpu.CompilerParams(
            dimension_semantics=("arbitrary",),
            vmem_limit_bytes=_vmem_limit(2 * d * tn * 4 + 2 * d * LANES * 4 + d * tn * 4)),
        name="adaln_mod",
    )(c.reshape(d, 1), w_ada, b_ada.reshape(1, n))


def _norm_mod_kernel(x_ref, gain_ref, scale_ref, shift_ref, o_ref):
    x = x_ref[...]
    ms = jnp.mean(x * x, axis=-1, keepdims=True)
    g = gain_ref[...] * (1.0 + scale_ref[...])
    o_ref[...] = (x * lax.rsqrt(ms + EPS) * g + shift_ref[...]).astype(o_ref.dtype)


def _norm_mod(x2, gain, scale, shift):
    s, d = x2.shape
    tm = 512
    row = pl.BlockSpec((1, d), lambda i: (0, 0))
    return pl.pallas_call(
        _norm_mod_kernel,
        out_shape=jax.ShapeDtypeStruct((s, d), jnp.bfloat16),
        grid=(s // tm,),
        in_specs=[pl.BlockSpec((tm, d), lambda i: (i, 0)), row, row, row],
        out_specs=pl.BlockSpec((tm, d), lambda i: (i, 0)),
        compiler_params=pltpu.CompilerParams(
            dimension_semantics=("arbitrary",),
            vmem_limit_bytes=_vmem_limit(2 * tm * d * (4 + 2) + 2 * tm * d * 4)),
        name="norm_mod",
    )(x2, gain, scale, shift)


CAST_BLOCK_COLS = 1024


class CastJob(NamedTuple):
    src: jax.Array
    col0: int
    cols: int


def _cast_specs(job, grid):
    g0, g1 = grid
    n_steps = g0 * g1
    rows = job.src.shape[0]
    cw = min(job.cols, CAST_BLOCK_COLS)
    n_cb = job.cols // cw
    rb = rows * n_cb // n_steps
    assert rb * n_steps == rows * n_cb and rb % 16 == 0 and job.col0 % cw == 0
    col_blk0 = job.col0 // cw

    def blk(i, j):
        s = i * g1 + j
        return s // n_cb, s % n_cb

    in_spec = pl.BlockSpec((rb, cw), lambda i, j: (blk(i, j)[0], col_blk0 + blk(i, j)[1]))
    out_spec = pl.BlockSpec((rb, cw), lambda i, j: blk(i, j))
    out_shape = jax.ShapeDtypeStruct((rows, job.cols), jnp.bfloat16)
    return in_spec, out_spec, out_shape, 2 * rb * cw * (4 + 2)


def _run_cast(src_ref, dst_ref):
    dst_ref[...] = src_ref[...].astype(dst_ref.dtype)


def _cast_cols(job):
    rows = job.src.shape[0]
    rb = 512
    cw = min(job.cols, CAST_BLOCK_COLS)
    assert rows % rb == 0 and job.cols % cw == 0 and job.col0 % cw == 0
    col_blk0 = job.col0 // cw
    return pl.pallas_call(
        _run_cast,
        out_shape=jax.ShapeDtypeStruct((rows, job.cols), jnp.bfloat16),
        grid=(rows // rb, job.cols // cw),
        in_specs=[pl.BlockSpec((rb, cw), lambda i, j: (i, col_blk0 + j))],
        out_specs=pl.BlockSpec((rb, cw), lambda i, j: (i, j)),
        compiler_params=pltpu.CompilerParams(
            dimension_semantics=("arbitrary", "arbitrary"),
            vmem_limit_bytes=_vmem_limit(2 * rb * cw * (4 + 2))),
        name="cast_cols",
    )(job.src)


def _ep_identity(acc):
    return acc


def _ep_silu(acc):
    return _silu(acc)


def _ep_sigmoid(acc):
    return _sigmoid(acc)


def _ep_residual(acc, x, gate):
    return x + gate * acc


def _matmul_kernel(*refs, epilogue, n_tiles, n_rows, has_cast):
    lhs_ref, rhs_ref = refs[0], refs[1]
    n_in = 2 + n_tiles + n_rows
    tile_refs = refs[2:2 + n_tiles]
    row_refs = refs[2 + n_tiles:n_in]
    o_ref = refs[n_in + has_cast]
    if has_cast:
        _run_cast(refs[n_in], refs[n_in + 2])
    acc = jnp.dot(lhs_ref[...], rhs_ref[...], preferred_element_type=jnp.float32)
    extras = [r[...] for r in tile_refs] + [r[...] for r in row_refs]
    o_ref[...] = epilogue(acc, *extras).astype(o_ref.dtype)


def _matmul(lhs, rhs, *, out_dtype, epilogue=_ep_identity, tiles=(), rows=(), tn=1024,
            cast=None, name):
    m, k = lhs.shape
    n_cols = rhs.shape[1]
    tm = 512 if tiles else 1024
    assert rhs.shape[0] == k and n_cols % tn == 0 and m % tm == 0
    grid = (m // tm, n_cols // tn)
    in_specs = [pl.BlockSpec((tm, k), lambda i, j: (i, 0)),
                pl.BlockSpec((k, tn), lambda i, j: (0, j))]
    in_specs += [pl.BlockSpec((tm, tn), lambda i, j: (i, j)) for _ in tiles]
    in_specs += [pl.BlockSpec((1, tn), lambda i, j: (0, j)) for _ in rows]
    out_bytes = jnp.dtype(out_dtype).itemsize
    tile_bytes = sum(t.dtype.itemsize for t in tiles)
    block_bytes = 2 * (tm * k * lhs.dtype.itemsize + k * tn * rhs.dtype.itemsize
                       + tm * tn * (out_bytes + tile_bytes)) + 2 * tm * tn * 4
    out_shape = jax.ShapeDtypeStruct((m, n_cols), out_dtype)
    out_spec = pl.BlockSpec((tm, tn), lambda i, j: (i, j))
    args = (lhs, rhs, *tiles, *rows)
    if cast is not None:
        c_in, c_out, c_shape, c_bytes = _cast_specs(cast, grid)
        in_specs.append(c_in)
        out_shape, out_spec = (out_shape, c_shape), (out_spec, c_out)
        args += (cast.src,)
        block_bytes += c_bytes
    return pl.pallas_call(
        functools.partial(_matmul_kernel, epilogue=epilogue, n_tiles=len(tiles),
                          n_rows=len(rows), has_cast=cast is not None),
        out_shape=out_shape,
        grid=grid,
        in_specs=in_specs,
        out_specs=out_spec,
        compiler_params=pltpu.CompilerParams(
            dimension_semantics=("arbitrary", "arbitrary"),
            vmem_limit_bytes=_vmem_limit(block_bytes)),
        name=name,
    )(*args)


def _merge_kernel(yr_ref, ya_ref, wr_ref, wa_ref, smr_ref, sma_ref, o_ref):
    p_rnn = jnp.dot(yr_ref[...], wr_ref[...], preferred_element_type=jnp.float32)
    p_attn = jnp.dot(ya_ref[...], wa_ref[...], preferred_element_type=jnp.float32)
    o_ref[...] = (smr_ref[...] * p_rnn + sma_ref[...] * p_attn).astype(o_ref.dtype)


def _merge(y_rnn, y_attn, w_rnn, w_attn, sm_rnn, sm_attn):
    m, k = y_rnn.shape
    n = w_rnn.shape[1]
    tm, tn = 512, 1024
    lhs = pl.BlockSpec((tm, k), lambda j, i: (i, 0))
    rhs = pl.BlockSpec((k, tn), lambda j, i: (0, j), pipeline_mode=pl.Buffered(1))
    tile = pl.BlockSpec((tm, tn), lambda j, i: (i, j))
    block_bytes = (2 * (2 * tm * k * 2 + tm * tn * (4 + 4 + 2)) + 2 * k * tn * 2
                   + 2 * tm * tn * 4)
    return pl.pallas_call(
        _merge_kernel,
        out_shape=jax.ShapeDtypeStruct((m, n), jnp.bfloat16),
        grid=(n // tn, m // tm),
        in_specs=[lhs, lhs, rhs, rhs, tile, tile],
        out_specs=tile,
        compiler_params=pltpu.CompilerParams(
            dimension_semantics=("arbitrary", "arbitrary"),
            vmem_limit_bytes=_vmem_limit(block_bytes)),
        name="merge_proj",
    )(y_rnn, y_attn, w_rnn, w_attn, sm_rnn, sm_attn)


def _out_norm_kernel(m_ref, w_ref, x_ref, gate_ref, gain_ref, o_ref, res_ref):
    j = pl.program_id(1)
    n_j = pl.num_programs(1)
    tn = w_ref.shape[1]
    acc = jnp.dot(m_ref[...], w_ref[...], preferred_element_type=jnp.float32)
    res_ref[j] = x_ref[...] + gate_ref[...] * acc

    @pl.when(j == n_j - 1)
    def _():
        n_tiles = res_ref.shape[0]
        ss = jnp.sum(res_ref[0] * res_ref[0], axis=-1, keepdims=True)
        for t in range(1, n_tiles):
            ss = ss + jnp.sum(res_ref[t] * res_ref[t], axis=-1, keepdims=True)
        inv = lax.rsqrt(ss / (n_tiles * tn) + EPS)
        for t in range(n_tiles):
            o_ref[:, t * tn:(t + 1) * tn] = res_ref[t] * inv * gain_ref[:, t * tn:(t + 1) * tn]


def _out_norm(merged, w, x2, gate, gain):
    m, k = merged.shape
    n = w.shape[1]
    tm, tn = 512, 1024
    block_bytes = (2 * (tm * k * 2 + k * tn * 2 + tm * tn * 4 + tm * n * 4) + tm * n * 4
                   + tm * tn * 4)
    return pl.pallas_call(
        _out_norm_kernel,
        out_shape=jax.ShapeDtypeStruct((m, n), jnp.float32),
        grid=(m // tm, n // tn),
        in_specs=[pl.BlockSpec((tm, k), lambda i, j: (i, 0)),
                  pl.BlockSpec((k, tn), lambda i, j: (0, j)),
                  pl.BlockSpec((tm, tn), lambda i, j: (i, j)),
                  pl.BlockSpec((1, tn), lambda i, j: (0, j)),
                  pl.BlockSpec((1, n), lambda i, j: (0, 0))],
        out_specs=pl.BlockSpec((tm, n), lambda i, j: (i, 0)),
        scratch_shapes=[pltpu.VMEM((n // tn, tm, tn), jnp.float32)],
        compiler_params=pltpu.CompilerParams(
            dimension_semantics=("arbitrary", "arbitrary"),
            vmem_limit_bytes=_vmem_limit(block_bytes)),
        name="out_norm",
    )(merged, w, x2, gate, gain)


RNN_SUB_ROWS = 512
RNN_BLOCK_ROWS = 4096
RNN_HALO_GROUPS = 4
SQRT_FLOOR = 1e-30


def _slabs(ref, rows, n_slabs):
    return jnp.concatenate([ref[l, rows, :] for l in range(n_slabs)], axis=1)


def _rglru_kernel(u_ref, sg_ref, cw_ref, cb_ref, wa_ref, ba_ref, wx_ref, bx_ref, lam_ref,
                  cast_src, y_ref, cast_dst, uperm, hbuf, pbuf, carry_ref, *, conv_width):
    ts = RNN_SUB_ROWS
    tc = ts // SUBLANES
    width = u_ref.shape[1]
    n_slabs = width // LANES
    data0 = RNN_HALO_GROUPS * SUBLANES
    taps = conv_width - 1
    assert taps < RNN_HALO_GROUPS and taps <= tc
    _run_cast(cast_src, cast_dst)

    @pl.when(pl.program_id(1) == 0)
    def _():
        uperm[...] = jnp.zeros_like(uperm)
        carry_ref[...] = jnp.zeros_like(carry_ref)

    neg = -lam_ref[...]
    softplus_neg = jnp.maximum(neg, 0.0) + jnp.log1p(jnp.exp(-jnp.abs(neg)))
    half_decay = (-0.5 * LRU_C) * softplus_neg
    half_ba = 0.5 * ba_ref[...]
    half_bx = 0.5 * bx_ref[...]
    cw = cw_ref[...]
    sub = lax.broadcasted_iota(jnp.int32, (SUBLANES, width), 0)

    def group(j):
        return pl.ds(data0 + j * SUBLANES, SUBLANES)

    def sub_tile(s, _):
        r0 = pl.multiple_of(s * ts, ts)
        prev = [_slabs(uperm, group(tc - k), n_slabs) for k in range(1, taps + 1)]
        for c in range(SUBLANES):
            for l in range(n_slabs):
                uperm[l, pl.ds(data0 + c, tc, stride=SUBLANES), :] = (
                    u_ref[pl.ds(r0 + c * tc, tc), l * LANES:(l + 1) * LANES])
        for k in range(1, taps + 1):
            cur = _slabs(uperm, group(tc - k), n_slabs)
            halo = jnp.where(sub == 0, pltpu.roll(prev[k - 1], 1, axis=0),
                             pltpu.roll(cur, 1, axis=0))
            for l in range(n_slabs):
                uperm[l, group(-k), :] = halo[:, l * LANES:(l + 1) * LANES]
        u = cb_ref[...]
        for k in range(conv_width):
            first = data0 - (taps - k) * SUBLANES
            u = u + cw[k:k + 1, :] * _slabs(uperm, pl.ds(first, ts), n_slabs)
        ub = u.astype(jnp.bfloat16)
        t_r = jnp.tanh(jnp.dot(ub, wa_ref[...], preferred_element_type=jnp.float32) + half_ba)
        t_i = jnp.tanh(jnp.dot(ub, wx_ref[...], preferred_element_type=jnp.float32) + half_bx)
        log_a = half_decay * t_r + half_decay
        a = jnp.exp(log_a)
        half_u = 0.5 * u
        gated_u = half_u * t_i + half_u
        one_m_a2 = 1.0 - a * a
        b = (one_m_a2 * lax.rsqrt(jnp.maximum(one_m_a2, SQRT_FLOOR))) * gated_u

        h_loc = b[0:SUBLANES]
        p_cum = a[0:SUBLANES]
        for j in range(tc):
            if j > 0:
                aj = a[j * SUBLANES:(j + 1) * SUBLANES]
                h_loc = aj * h_loc + b[j * SUBLANES:(j + 1) * SUBLANES]
                p_cum = aj * p_cum
            pbuf[j * SUBLANES:(j + 1) * SUBLANES, :] = p_cum
            for l in range(n_slabs):
                hbuf[l, j * SUBLANES:(j + 1) * SUBLANES, :] = h_loc[:, l * LANES:(l + 1) * LANES]

        a_inc, b_inc = p_cum, h_loc
        shift = 1
        while shift < SUBLANES:
            keep = sub >= shift
            a_prev = jnp.where(keep, pltpu.roll(a_inc, shift, axis=0), 1.0)
            b_prev = jnp.where(keep, pltpu.roll(b_inc, shift, axis=0), 0.0)
            b_inc = a_inc * b_prev + b_inc
            a_inc = a_inc * a_prev
            shift *= 2
        h_in = jnp.broadcast_to(carry_ref[...], (SUBLANES, width))
        chunk_end = b_inc + a_inc * h_in
        entry = jnp.where(sub == 0, h_in, pltpu.roll(chunk_end, 1, axis=0))
        carry_ref[...] = chunk_end[SUBLANES - 1:SUBLANES, :]

        for j in range(tc):
            rows = slice(j * SUBLANES, (j + 1) * SUBLANES)
            h = _slabs(hbuf, rows, n_slabs) + pbuf[rows, :] * entry
            for l in range(n_slabs):
                hbuf[l, rows, :] = h[:, l * LANES:(l + 1) * LANES]
        for c in range(SUBLANES):
            rows = pl.ds(r0 + c * tc, tc)
            h_nat = _slabs(hbuf, pl.ds(c, tc, stride=SUBLANES), n_slabs)
            y_ref[rows, :] = (h_nat * sg_ref[rows, :]).astype(y_ref.dtype)
        return 0

    lax.fori_loop(0, u_ref.shape[0] // ts, sub_tile, 0, unroll=2)


def _rglru(u_rnn, sg_rnn, conv_w, conv_b, w_gate_a, b_gate_a, w_gate_x, b_gate_x, lam, cast):
    s, d = u_rnn.shape
    nb, bw, _ = w_gate_a.shape
    conv_width = conv_w.shape[0]
    tb = RNN_BLOCK_ROWS
    seq = pl.BlockSpec((tb, bw), lambda c, t: (t, c))
    vec = pl.BlockSpec((1, bw), lambda c, t: (0, c))
    gate_w = pl.BlockSpec((None, bw, bw), lambda c, t: (c, 0, 0))
    gate_b = pl.BlockSpec((None, 1, bw), lambda c, t: (c, 0, 0))
    block_bytes = (2 * tb * bw * (4 + 4 + 2) + 4 * bw * bw * 2
                   + 3 * (RNN_SUB_ROWS + RNN_HALO_GROUPS * SUBLANES) * bw * 4)
    grid = (nb, s // tb)
    c_in, c_out, c_shape, c_bytes = _cast_specs(cast, grid)
    return pl.pallas_call(
        functools.partial(_rglru_kernel, conv_width=conv_width),
        out_shape=(jax.ShapeDtypeStruct((s, d), jnp.bfloat16), c_shape),
        grid=grid,
        in_specs=[seq, seq,
                  pl.BlockSpec((conv_width, bw), lambda c, t: (0, c)), vec,
                  gate_w, gate_b, gate_w, gate_b, vec, c_in],
        out_specs=(seq, c_out),
        scratch_shapes=[
            pltpu.VMEM((bw // LANES, RNN_SUB_ROWS + RNN_HALO_GROUPS * SUBLANES, LANES),
                       jnp.float32),
            pltpu.VMEM((bw // LANES, RNN_SUB_ROWS, LANES), jnp.float32),
            pltpu.VMEM((RNN_SUB_ROWS, bw), jnp.float32),
            pltpu.VMEM((1, bw), jnp.float32)],
        compiler_params=pltpu.CompilerParams(
            dimension_semantics=("arbitrary", "arbitrary"),
            vmem_limit_bytes=_vmem_limit(block_bytes + c_bytes + 16 * 1024 * 1024)),
        name="rglru",
    )(u_rnn, sg_rnn, conv_w, conv_b.reshape(1, d),
      (0.5 * w_gate_a).astype(jnp.bfloat16), b_gate_a.reshape(nb, 1, bw),
      (0.5 * w_gate_x).astype(jnp.bfloat16), b_gate_x.reshape(nb, 1, bw), lam.reshape(1, d),
      cast.src)


def _swap_halves(v):
    half = v.shape[1] // 2
    return jnp.concatenate([v[:, half:], v[:, :half]], axis=1)


def _attn_block(first, sink_ref, q_ref, k_prev, k_cur, sg_ref, o_ref, *, n_kv, group, head_dim):
    bq = q_ref.shape[0]
    pair_w = 2 * head_dim
    pairs_per_group = group // 2
    d_kv = n_kv * head_dim

    q_pos = lax.broadcasted_iota(jnp.int32, (bq, 2 * bq), 0)
    k_pos = lax.broadcasted_iota(jnp.int32, (bq, 2 * bq), 1)
    dist = q_pos + bq - k_pos
    valid = (dist >= 0) & (dist < WINDOW) & ((k_pos >= bq) | jnp.logical_not(first))
    lane = lax.broadcasted_iota(jnp.int32, (bq, pair_w), 1)
    lo_q = lane < head_dim
    lane_kv = lax.broadcasted_iota(jnp.int32, (2 * bq, pair_w), 1)
    lo_kv = lane_kv < head_dim
    scale = head_dim ** -0.5
    zero_q = jnp.zeros((bq, pair_w), q_ref.dtype)

    for kv_pair in range(n_kv // 2):
        cols = slice(kv_pair * pair_w, (kv_pair + 1) * pair_w)
        k_two = jnp.concatenate([k_prev[:, cols], k_cur[:, cols]], axis=0)
        v_cols = slice(d_kv + kv_pair * pair_w, d_kv + (kv_pair + 1) * pair_w)
        v_two = jnp.concatenate([k_prev[:, v_cols], k_cur[:, v_cols]], axis=0)
        k_swapped = _swap_halves(k_two)
        v_swapped = _swap_halves(v_two)
        for parity in range(2):
            hk = 2 * kv_pair + parity
            if parity == 0:
                k_dup = jnp.where(lo_kv, k_two, k_swapped)
                v_dup = jnp.where(lo_kv, v_two, v_swapped)
            else:
                k_dup = jnp.where(lo_kv, k_swapped, k_two)
                v_dup = jnp.where(lo_kv, v_swapped, v_two)
            k_dup = k_dup * jnp.asarray(scale, k_dup.dtype)
            q_rows = []
            for p in range(pairs_per_group):
                pair = hk * pairs_per_group + p
                qp = q_ref[:, pair * pair_w:(pair + 1) * pair_w]
                q_rows.append(jnp.where(lo_q, qp, zero_q))
                q_rows.append(jnp.where(lo_q, zero_q, qp))
            q_stack = jnp.concatenate(q_rows, axis=0)
            s_all = lax.dot_general(q_stack, k_dup, (((1,), (1,)), ((), ())),
                                    preferred_element_type=jnp.float32)
            p_rows, inv_rows = [], []
            for g in range(group):
                sink = sink_ref[hk * group + g]
                sc = jnp.where(valid, s_all[g * bq:(g + 1) * bq, :], -jnp.inf)
                m = jnp.maximum(jnp.max(sc, axis=-1, keepdims=True), sink)
                e = jnp.exp(sc - m)
                denom = jnp.sum(e, axis=-1, keepdims=True) + jnp.exp(sink - m)
                p_rows.append(e.astype(v_dup.dtype))
                inv_rows.append(1.0 / denom)
            p_stack = jnp.concatenate(p_rows, axis=0)
            o_all = jnp.dot(p_stack, v_dup, preferred_element_type=jnp.float32)
            for p in range(pairs_per_group):
                pair = hk * pairs_per_group + p
                o_even = o_all[(2 * p) * bq:(2 * p + 1) * bq, :] * inv_rows[2 * p]
                o_odd = o_all[(2 * p + 1) * bq:(2 * p + 2) * bq, :] * inv_rows[2 * p + 1]
                o_pair = jnp.where(lo_q, o_even, o_odd)
                c0 = pair * pair_w
                o_ref[:, c0:c0 + pair_w] = (o_pair * sg_ref[:, c0:c0 + pair_w]).astype(o_ref.dtype)


def _attn_kernel(sink_ref, q_ref, kv_ref, kvp_ref, sg_ref, cast_src, o_ref, cast_dst, *,
                 n_kv, group, head_dim):
    _run_cast(cast_src, cast_dst)
    bq = BLOCK_Q
    n = pl.program_id(0)
    for b in range(q_ref.shape[0] // bq):
        rows = pl.ds(b * bq, bq)
        k_prev = kvp_ref if b == 0 else kv_ref.at[pl.ds((b - 1) * bq, bq), :]
        first = (n == 0) if b == 0 else jnp.bool_(False)
        _attn_block(first, sink_ref, q_ref.at[rows, :], k_prev, kv_ref.at[rows, :],
                    sg_ref.at[rows, :], o_ref.at[rows, :], n_kv=n_kv, group=group,
                    head_dim=head_dim)


def _attention(q, kv, sg_attn, sinks, cast, *, n_kv, head_dim):
    s, d_attn = q.shape
    n_q = d_attn // head_dim
    group = n_q // n_kv
    assert head_dim * 2 == LANES and group % 2 == 0 and n_kv % 2 == 0
    bq = BLOCK_Q
    rows = ATTN_BLOCKS_PER_STEP * bq
    d_kv2 = kv.shape[1]
    grid = (s // rows, 1)
    c_in, c_out, c_shape, c_bytes = _cast_specs(cast, grid)
    block_bytes = 2 * rows * (d_attn * (2 + 4 + 2) + d_kv2 * 2) + 2 * bq * d_kv2 * 2 + c_bytes
    seq = pl.BlockSpec((rows, d_attn), lambda n, _: (n, 0))
    return pl.pallas_call(
        functools.partial(_attn_kernel, n_kv=n_kv, group=group, head_dim=head_dim),
        out_shape=(jax.ShapeDtypeStruct((s, d_attn), jnp.bfloat16), c_shape),
        grid=grid,
        in_specs=[pl.BlockSpec(memory_space=pltpu.SMEM),
                  seq,
                  pl.BlockSpec((rows, d_kv2), lambda n, _: (n, 0)),
                  pl.BlockSpec((bq, d_kv2),
                               lambda n, _: (jnp.maximum(n * ATTN_BLOCKS_PER_STEP - 1, 0), 0)),
                  seq, c_in],
        out_specs=(seq, c_out),
        compiler_params=pltpu.CompilerParams(
            dimension_semantics=("arbitrary", "arbitrary"),
            vmem_limit_bytes=_vmem_limit(block_bytes + 16 * 1024 * 1024)),
        name="swa_attn",
    )(sinks, q, kv, kv, sg_attn, cast.src)


def kernel(x, c, w_ada, b_ada, norm_gain, w_in, conv_w, conv_b, w_gate_a, b_gate_a, w_gate_x,
           b_gate_x, lru_lambda, attn_sinks, w_rnn_proj, w_attn_proj, w_out, final_norm_gain):
    bsz, seq, d = x.shape
    depth = w_in.shape[0]
    d_rnn = w_rnn_proj.shape[1]
    d_attn = w_attn_proj.shape[1]
    n_q = attn_sinks.shape[1]
    head_dim = d_attn // n_q
    d_kv = (w_in.shape[2] - 2 * d_rnn - 2 * d_attn - 2 * d) // 2
    n_kv = d_kv // head_dim
    assert bsz == 1

    xs = x.reshape(bsz * seq, d)
    bf = jnp.bfloat16
    for l in range(depth):
        mod = _modulation(c, w_ada[l], b_ada[l])
        shift, scale, gate = mod[:, :d], mod[:, d:2 * d], mod[:, 2 * d:]
        h = _norm_mod(xs, norm_gain[l].reshape(1, d), scale, shift)

        wl = w_in[l]
        c_u, c_g, c_q, c_kv = 0, d_rnn, 2 * d_rnn, 2 * d_rnn + d_attn
        c_ga = c_kv + 2 * d_kv
        c_mr, c_ma = c_ga + d_attn, c_ga + d_attn + d
        proj = functools.partial(_matmul, h)
        w_kv = _cast_cols(CastJob(wl, c_kv, 2 * d_kv))
        kv, w_u = proj(w_kv, out_dtype=bf, cast=CastJob(wl, c_u, d_rnn), name="proj_kv")
        u_rnn, w_g = proj(w_u, out_dtype=jnp.float32, cast=CastJob(wl, c_g, d_rnn),
                          name="proj_u")
        sg_rnn, w_q = proj(w_g, out_dtype=jnp.float32, epilogue=_ep_silu,
                           cast=CastJob(wl, c_q, d_attn), name="proj_g_rnn")
        q, w_ga = proj(w_q, out_dtype=bf, cast=CastJob(wl, c_ga, d_attn), name="proj_q")
        sg_attn, w_mr = proj(w_ga, out_dtype=jnp.float32, epilogue=_ep_silu,
                             cast=CastJob(wl, c_mr, d), name="proj_g_attn")
        sm_rnn, w_ma = proj(w_mr, out_dtype=jnp.float32, epilogue=_ep_sigmoid,
                            cast=CastJob(wl, c_ma, d), name="proj_m_rnn")
        sm_attn, w_rp = proj(w_ma, out_dtype=jnp.float32, epilogue=_ep_sigmoid,
                             cast=CastJob(w_rnn_proj[l], 0, d), name="proj_m_attn")

        y_rnn, w_ap = _rglru(u_rnn, sg_rnn, conv_w[l], conv_b[l], w_gate_a[l], b_gate_a[l],
                             w_gate_x[l], b_gate_x[l], lru_lambda[l],
                             CastJob(w_attn_proj[l], 0, d))
        y_attn, w_o = _attention(q, kv, sg_attn, attn_sinks[l], CastJob(w_out[l], 0, d),
                                 n_kv=n_kv, head_dim=head_dim)

        merged = _merge(y_rnn, y_attn, w_rp, w_ap, sm_rnn, sm_attn)
        if l + 1 < depth:
            xs = _matmul(merged, w_o, out_dtype=jnp.float32, epilogue=_ep_residual,
                         tiles=(xs,), rows=(gate,), name="out_proj")
        else:
            xs = _out_norm(merged, w_o, xs, gate, final_norm_gain.reshape(1, d))
    return xs.reshape(bsz, seq, d)
```

```python
import functools
from typing import NamedTuple

import jax
import jax.numpy as jnp
from jax import lax
from jax.experimental import pallas as pl
from jax.experimental.pallas import tpu as pltpu

EPS = 1e-6
LRU_C = 8.0
WINDOW = 128
BLOCK_Q = 128
ATTN_BLOCKS_PER_STEP = 2

V7X_VMEM_BYTES = 64 * 1024 * 1024
VMEM_HEADROOM_BYTES = 6 * 1024 * 1024
LANES = 128
SUBLANES = 8


def _vmem_limit(block_bytes):
    need = int(block_bytes) + VMEM_HEADROOM_BYTES
    assert need <= V7X_VMEM_BYTES - 2 * 1024 * 1024, need
    return need


def _sigmoid(v):
    return 0.5 * jnp.tanh(0.5 * v) + 0.5


def _silu(v):
    return v * _sigmoid(v)


def _mod_kernel(c_ref, w_ref, b_ref, o_ref):
    c_act = _silu(c_ref[...])
    o_ref[...] = jnp.sum(w_ref[...] * c_act, axis=0, keepdims=True) + b_ref[...]


def _modulation(c, w_ada, b_ada):
    d, n = w_ada.shape
    tn = 512
    return pl.pallas_call(
        _mod_kernel,
        out_shape=jax.ShapeDtypeStruct((1, n), jnp.float32),
        grid=(n // tn,),
        in_specs=[pl.BlockSpec((d, 1), lambda j: (0, 0)),
                  pl.BlockSpec((d, tn), lambda j: (0, j)),
                  pl.BlockSpec((1, tn), lambda j: (0, j))],
        out_specs=pl.BlockSpec((1, tn), lambda j: (0, j)),
        compiler_params=pltpu.CompilerParams(
            dimension_semantics=("arbitrary",),
            vmem_limit_bytes=_vmem_limit(2 * d * tn * 4 + 2 * d * LANES * 4 + d * tn * 4)),
        name="adaln_mod",
    )(c.reshape(d, 1), w_ada, b_ada.reshape(1, n))


def _norm_proj_kernel(x_ref, gain_ref, scale_ref, shift_ref, w_ref, cast_src,
                      h_ref, o_ref, cast_dst):
    _run_cast(cast_src, cast_dst)
    x = x_ref[...]
    ms = jnp.mean(x * x, axis=-1, keepdims=True)
    g = gain_ref[...] * (1.0 + scale_ref[...])
    h = (x * lax.rsqrt(ms + EPS) * g + shift_ref[...]).astype(h_ref.dtype)
    h_ref[...] = h
    o_ref[...] = jnp.dot(h, w_ref[...], preferred_element_type=jnp.float32).astype(o_ref.dtype)


def _norm_proj(x2, gain, scale, shift, w, cast):
    s, d = x2.shape
    n = w.shape[1]
    tm = 256
    grid = (s // tm, 1)
    c_in, c_out, c_shape, c_bytes = _cast_specs(cast, grid)
    row = pl.BlockSpec((1, d), lambda i, _: (0, 0))
    seq = pl.BlockSpec((tm, d), lambda i, _: (i, 0))
    block_bytes = (2 * tm * d * (4 + 2) + 2 * tm * d * 4 + d * n * 2 + 2 * tm * n * 2
                   + tm * n * 4 + c_bytes)
    return pl.pallas_call(
        _norm_proj_kernel,
        out_shape=(jax.ShapeDtypeStruct((s, d), jnp.bfloat16),
                   jax.ShapeDtypeStruct((s, n), jnp.bfloat16), c_shape),
        grid=grid,
        in_specs=[seq, row, row, row,
                  pl.BlockSpec((d, n), lambda i, _: (0, 0), pipeline_mode=pl.Buffered(1)),
                  c_in],
        out_specs=(seq, pl.BlockSpec((tm, n), lambda i, _: (i, 0)), c_out),
        compiler_params=pltpu.CompilerParams(
            dimension_semantics=("arbitrary", "arbitrary"),
            vmem_limit_bytes=_vmem_limit(block_bytes)),
        name="norm_proj_kv",
    )(x2, gain, scale, shift, w, cast.src)


CAST_BLOCK_COLS = 1024


class CastJob(NamedTuple):
    src: jax.Array
    col0: int
    cols: int


def _cast_specs(job, grid):
    g0, g1 = grid
    n_steps = g0 * g1
    rows = job.src.shape[0]
    cw = min(job.cols, CAST_BLOCK_COLS)
    n_cb = job.cols // cw
    rb = rows * n_cb // n_steps
    assert rb * n_steps == rows * n_cb and rb % 16 == 0 and job.col0 % cw == 0
    col_blk0 = job.col0 // cw

    def blk(i, j):
        s = i * g1 + j
        return s // n_cb, s % n_cb

    in_spec = pl.BlockSpec((rb, cw), lambda i, j: (blk(i, j)[0], col_blk0 + blk(i, j)[1]))
    out_spec = pl.BlockSpec((rb, cw), lambda i, j: blk(i, j))
    out_shape = jax.ShapeDtypeStruct((rows, job.cols), jnp.bfloat16)
    return in_spec, out_spec, out_shape, 2 * rb * cw * (4 + 2)


def _run_cast(src_ref, dst_ref):
    dst_ref[...] = src_ref[...].astype(dst_ref.dtype)


def _cast_cols(job):
    rows = job.src.shape[0]
    rb = 512
    cw = min(job.cols, CAST_BLOCK_COLS)
    assert rows % rb == 0 and job.cols % cw == 0 and job.col0 % cw == 0
    col_blk0 = job.col0 // cw
    return pl.pallas_call(
        _run_cast,
        out_shape=jax.ShapeDtypeStruct((rows, job.cols), jnp.bfloat16),
        grid=(rows // rb, job.cols // cw),
        in_specs=[pl.BlockSpec((rb, cw), lambda i, j: (i, col_blk0 + j))],
        out_specs=pl.BlockSpec((rb, cw), lambda i, j: (i, j)),
        compiler_params=pltpu.CompilerParams(
            dimension_semantics=("arbitrary", "arbitrary"),
            vmem_limit_bytes=_vmem_limit(2 * rb * cw * (4 + 2))),
        name="cast_cols",
    )(job.src)


def _ep_identity(acc):
    return acc


def _ep_silu(acc):
    return _silu(acc)


def _ep_sigmoid(acc):
    return _sigmoid(acc)


def _ep_residual(acc, x, gate):
    return x + gate * acc


def _matmul_kernel(*refs, epilogue, n_tiles, n_rows, has_cast):
    lhs_ref, rhs_ref = refs[0], refs[1]
    n_in = 2 + n_tiles + n_rows
    tile_refs = refs[2:2 + n_tiles]
    row_refs = refs[2 + n_tiles:n_in]
    o_ref = refs[n_in + has_cast]
    if has_cast:
        _run_cast(refs[n_in], refs[n_in + 2])
    acc = jnp.dot(lhs_ref[...], rhs_ref[...], preferred_element_type=jnp.float32)
    extras = [r[...] for r in tile_refs] + [r[...] for r in row_refs]
    o_ref[...] = epilogue(acc, *extras).astype(o_ref.dtype)


def _matmul(lhs, rhs, *, out_dtype, epilogue=_ep_identity, tiles=(), rows=(), tn=1024,
            cast=None, name):
    m, k = lhs.shape
    n_cols = rhs.shape[1]
    tm = 512 if tiles else 1024
    assert rhs.shape[0] == k and n_cols % tn == 0 and m % tm == 0
    grid = (m // tm, n_cols // tn)
    in_specs = [pl.BlockSpec((tm, k), lambda i, j: (i, 0)),
                pl.BlockSpec((k, tn), lambda i, j: (0, j))]
    in_specs += [pl.BlockSpec((tm, tn), lambda i, j: (i, j)) for _ in tiles]
    in_specs += [pl.BlockSpec((1, tn), lambda i, j: (0, j)) for _ in rows]
    out_bytes = jnp.dtype(out_dtype).itemsize
    tile_bytes = sum(t.dtype.itemsize for t in tiles)
    block_bytes = 2 * (tm * k * lhs.dtype.itemsize + k * tn * rhs.dtype.itemsize
                       + tm * tn * (out_bytes + tile_bytes)) + 2 * tm * tn * 4
    out_shape = jax.ShapeDtypeStruct((m, n_cols), out_dtype)
    out_spec = pl.BlockSpec((tm, tn), lambda i, j: (i, j))
    args = (lhs, rhs, *tiles, *rows)
    if cast is not None:
        c_in, c_out, c_shape, c_bytes = _cast_specs(cast, grid)
        in_specs.append(c_in)
        out_shape, out_spec = (out_shape, c_shape), (out_spec, c_out)
        args += (cast.src,)
        block_bytes += c_bytes
    return pl.pallas_call(
        functools.partial(_matmul_kernel, epilogue=epilogue, n_tiles=len(tiles),
                          n_rows=len(rows), has_cast=cast is not None),
        out_shape=out_shape,
        grid=grid,
        in_specs=in_specs,
        out_specs=out_spec,
        compiler_params=pltpu.CompilerParams(
            dimension_semantics=("arbitrary", "arbitrary"),
            vmem_limit_bytes=_vmem_limit(block_bytes)),
        name=name,
    )(*args)


def _merge_kernel(yr_ref, ya_ref, wr_ref, wa_ref, smr_ref, sma_ref, o_ref):
    p_rnn = jnp.dot(yr_ref[...], wr_ref[...], preferred_element_type=jnp.float32)
    p_attn = jnp.dot(ya_ref[...], wa_ref[...], preferred_element_type=jnp.float32)
    o_ref[...] = (smr_ref[...] * p_rnn + sma_ref[...] * p_attn).astype(o_ref.dtype)


def _merge(y_rnn, y_attn, w_rnn, w_attn, sm_rnn, sm_attn):
    m, k = y_rnn.shape
    n = w_rnn.shape[1]
    tm, tn = 512, 1024
    lhs = pl.BlockSpec((tm, k), lambda j, i: (i, 0))
    rhs = pl.BlockSpec((k, tn), lambda j, i: (0, j), pipeline_mode=pl.Buffered(1))
    tile = pl.BlockSpec((tm, tn), lambda j, i: (i, j))
    block_bytes = (2 * (2 * tm * k * 2 + tm * tn * (4 + 4 + 2)) + 2 * k * tn * 2
                   + 2 * tm * tn * 4)
    return pl.pallas_call(
        _merge_kernel,
        out_shape=jax.ShapeDtypeStruct((m, n), jnp.bfloat16),
        grid=(n // tn, m // tm),
        in_specs=[lhs, lhs, rhs, rhs, tile, tile],
        out_specs=tile,
        compiler_params=pltpu.CompilerParams(
            dimension_semantics=("arbitrary", "arbitrary"),
            vmem_limit_bytes=_vmem_limit(block_bytes)),
        name="merge_proj",
    )(y_rnn, y_attn, w_rnn, w_attn, sm_rnn, sm_attn)


def _out_norm_kernel(m_ref, w_ref, x_ref, gate_ref, gain_ref, o_ref, res_ref):
    j = pl.program_id(1)
    n_j = pl.num_programs(1)
    tn = w_ref.shape[1]
    acc = jnp.dot(m_ref[...], w_ref[...], preferred_element_type=jnp.float32)
    res_ref[j] = x_ref[...] + gate_ref[...] * acc

    @pl.when(j == n_j - 1)
    def _():
        n_tiles = res_ref.shape[0]
        ss = jnp.sum(res_ref[0] * res_ref[0], axis=-1, keepdims=True)
        for t in range(1, n_tiles):
            ss = ss + jnp.sum(res_ref[t] * res_ref[t], axis=-1, keepdims=True)
        inv = lax.rsqrt(ss / (n_tiles * tn) + EPS)
        for t in range(n_tiles):
            o_ref[:, t * tn:(t + 1) * tn] = res_ref[t] * inv * gain_ref[:, t * tn:(t + 1) * tn]


def _out_norm(merged, w, x2, gate, gain):
    m, k = merged.shape
    n = w.shape[1]
    tm, tn = 512, 1024
    block_bytes = (2 * (tm * k * 2 + k * tn * 2 + tm * tn * 4 + tm * n * 4) + tm * n * 4
                   + tm * tn * 4)
    return pl.pallas_call(
        _out_norm_kernel,
        out_shape=jax.ShapeDtypeStruct((m, n), jnp.float32),
        grid=(m // tm, n // tn),
        in_specs=[pl.BlockSpec((tm, k), lambda i, j: (i, 0)),
                  pl.BlockSpec((k, tn), lambda i, j: (0, j)),
                  pl.BlockSpec((tm, tn), lambda i, j: (i, j)),
                  pl.BlockSpec((1, tn), lambda i, j: (0, j)),
                  pl.BlockSpec((1, n), lambda i, j: (0, 0))],
        out_specs=pl.BlockSpec((tm, n), lambda i, j: (i, 0)),
        scratch_shapes=[pltpu.VMEM((n // tn, tm, tn), jnp.float32)],
        compiler_params=pltpu.CompilerParams(
            dimension_semantics=("arbitrary", "arbitrary"),
            vmem_limit_bytes=_vmem_limit(block_bytes)),
        name="out_norm",
    )(merged, w, x2, gate, gain)


RNN_SUB_ROWS = 512
RNN_BLOCK_ROWS = 4096
RNN_HALO_GROUPS = 4
SQRT_FLOOR = 1e-30


def _slabs(ref, rows, n_slabs):
    return jnp.concatenate([ref[l, rows, :] for l in range(n_slabs)], axis=1)


def _rglru_kernel(u_ref, sg_ref, cw_ref, cb_ref, wa_ref, ba_ref, wx_ref, bx_ref, lam_ref,
                  cast_src, y_ref, cast_dst, uperm, hbuf, pbuf, carry_ref, *, conv_width):
    ts = RNN_SUB_ROWS
    tc = ts // SUBLANES
    width = u_ref.shape[1]
    n_slabs = width // LANES
    data0 = RNN_HALO_GROUPS * SUBLANES
    taps = conv_width - 1
    assert taps < RNN_HALO_GROUPS and taps <= tc
    _run_cast(cast_src, cast_dst)

    @pl.when(pl.program_id(1) == 0)
    def _():
        uperm[...] = jnp.zeros_like(uperm)
        carry_ref[...] = jnp.zeros_like(carry_ref)

    neg = -lam_ref[...]
    softplus_neg = jnp.maximum(neg, 0.0) + jnp.log1p(jnp.exp(-jnp.abs(neg)))
    half_decay = (-0.5 * LRU_C) * softplus_neg
    half_ba = 0.5 * ba_ref[...]
    half_bx = 0.5 * bx_ref[...]
    cw = cw_ref[...]
    sub = lax.broadcasted_iota(jnp.int32, (SUBLANES, width), 0)

    def group(j):
        return pl.ds(data0 + j * SUBLANES, SUBLANES)

    def sub_tile(s, _):
        r0 = pl.multiple_of(s * ts, ts)
        prev = [_slabs(uperm, group(tc - k), n_slabs) for k in range(1, taps + 1)]
        for c in range(SUBLANES):
            for l in range(n_slabs):
                uperm[l, pl.ds(data0 + c, tc, stride=SUBLANES), :] = (
                    u_ref[pl.ds(r0 + c * tc, tc), l * LANES:(l + 1) * LANES])
        for k in range(1, taps + 1):
            cur = _slabs(uperm, group(tc - k), n_slabs)
            halo = jnp.where(sub == 0, pltpu.roll(prev[k - 1], 1, axis=0),
                             pltpu.roll(cur, 1, axis=0))
            for l in range(n_slabs):
                uperm[l, group(-k), :] = halo[:, l * LANES:(l + 1) * LANES]
        u = cb_ref[...]
        for k in range(conv_width):
            first = data0 - (taps - k) * SUBLANES
            u = u + cw[k:k + 1, :] * _slabs(uperm, pl.ds(first, ts), n_slabs)
        ub = u.astype(jnp.bfloat16)
        t_r = jnp.tanh(jnp.dot(ub, wa_ref[...], preferred_element_type=jnp.float32) + half_ba)
        t_i = jnp.tanh(jnp.dot(ub, wx_ref[...], preferred_element_type=jnp.float32) + half_bx)
        log_a = half_decay * t_r + half_decay
        a = jnp.exp(log_a)
        half_u = 0.5 * u
        gated_u = half_u * t_i + half_u
        one_m_a2 = 1.0 - a * a
        b = (one_m_a2 * lax.rsqrt(jnp.maximum(one_m_a2, SQRT_FLOOR))) * gated_u

        h_loc = b[0:SUBLANES]
        p_cum = a[0:SUBLANES]
        for j in range(tc):
            if j > 0:
                aj = a[j * SUBLANES:(j + 1) * SUBLANES]
                h_loc = aj * h_loc + b[j * SUBLANES:(j + 1) * SUBLANES]
                p_cum = aj * p_cum
            pbuf[j * SUBLANES:(j + 1) * SUBLANES, :] = p_cum
            for l in range(n_slabs):
                hbuf[l, j * SUBLANES:(j + 1) * SUBLANES, :] = h_loc[:, l * LANES:(l + 1) * LANES]

        a_inc, b_inc = p_cum, h_loc
        shift = 1
        while shift < SUBLANES:
            keep = sub >= shift
            a_prev = jnp.where(keep, pltpu.roll(a_inc, shift, axis=0), 1.0)
            b_prev = jnp.where(keep, pltpu.roll(b_inc, shift, axis=0), 0.0)
            b_inc = a_inc * b_prev + b_inc
            a_inc = a_inc * a_prev
            shift *= 2
        h_in = jnp.broadcast_to(carry_ref[...], (SUBLANES, width))
        chunk_end = b_inc + a_inc * h_in
        entry = jnp.where(sub == 0, h_in, pltpu.roll(chunk_end, 1, axis=0))
        carry_ref[...] = chunk_end[SUBLANES - 1:SUBLANES, :]

        for j in range(tc):
            rows = slice(j * SUBLANES, (j + 1) * SUBLANES)
            h = _slabs(hbuf, rows, n_slabs) + pbuf[rows, :] * entry
            for l in range(n_slabs):
                hbuf[l, rows, :] = h[:, l * LANES:(l + 1) * LANES]
        for c in range(SUBLANES):
            rows = pl.ds(r0 + c * tc, tc)
            h_nat = _slabs(hbuf, pl.ds(c, tc, stride=SUBLANES), n_slabs)
            y_ref[rows, :] = (h_nat * sg_ref[rows, :]).astype(y_ref.dtype)
        return 0

    lax.fori_loop(0, u_ref.shape[0] // ts, sub_tile, 0, unroll=2)


def _rglru(u_rnn, sg_rnn, conv_w, conv_b, w_gate_a, b_gate_a, w_gate_x, b_gate_x, lam, cast):
    s, d = u_rnn.shape
    nb, bw, _ = w_gate_a.shape
    conv_width = conv_w.shape[0]
    tb = RNN_BLOCK_ROWS
    seq = pl.BlockSpec((tb, bw), lambda c, t: (t, c))
    vec = pl.BlockSpec((1, bw), lambda c, t: (0, c))
    gate_w = pl.BlockSpec((None, bw, bw), lambda c, t: (c, 0, 0))
    gate_b = pl.BlockSpec((None, 1, bw), lambda c, t: (c, 0, 0))
    block_bytes = (2 * tb * bw * (4 + 4 + 2) + 4 * bw * bw * 2
                   + 3 * (RNN_SUB_ROWS + RNN_HALO_GROUPS * SUBLANES) * bw * 4)
    grid = (nb, s // tb)
    c_in, c_out, c_shape, c_bytes = _cast_specs(cast, grid)
    return pl.pallas_call(
        functools.partial(_rglru_kernel, conv_width=conv_width),
        out_shape=(jax.ShapeDtypeStruct((s, d), jnp.bfloat16), c_shape),
        grid=grid,
        in_specs=[seq, seq,
                  pl.BlockSpec((conv_width, bw), lambda c, t: (0, c)), vec,
                  gate_w, gate_b, gate_w, gate_b, vec, c_in],
        out_specs=(seq, c_out),
        scratch_shapes=[
            pltpu.VMEM((bw // LANES, RNN_SUB_ROWS + RNN_HALO_GROUPS * SUBLANES, LANES),
                       jnp.float32),
            pltpu.VMEM((bw // LANES, RNN_SUB_ROWS, LANES), jnp.float32),
            pltpu.VMEM((RNN_SUB_ROWS, bw), jnp.float32),
            pltpu.VMEM((1, bw), jnp.float32)],
        compiler_params=pltpu.CompilerParams(
            dimension_semantics=("arbitrary", "arbitrary"),
            vmem_limit_bytes=_vmem_limit(block_bytes + c_bytes + 16 * 1024 * 1024)),
        name="rglru",
    )(u_rnn, sg_rnn, conv_w, conv_b.reshape(1, d),
      (0.5 * w_gate_a).astype(jnp.bfloat16), b_gate_a.reshape(nb, 1, bw),
      (0.5 * w_gate_x).astype(jnp.bfloat16), b_gate_x.reshape(nb, 1, bw), lam.reshape(1, d),
      cast.src)


def _swap_halves(v):
    half = v.shape[1] // 2
    return jnp.concatenate([v[:, half:], v[:, :half]], axis=1)


def _attn_block(first, sink_ref, q_ref, k_prev, k_cur, sg_ref, o_ref, *, n_kv, group, head_dim):
    bq = q_ref.shape[0]
    pair_w = 2 * head_dim
    pairs_per_group = group // 2
    d_kv = n_kv * head_dim

    q_pos = lax.broadcasted_iota(jnp.int32, (bq, 2 * bq), 0)
    k_pos = lax.broadcasted_iota(jnp.int32, (bq, 2 * bq), 1)
    dist = q_pos + bq - k_pos
    valid = (dist >= 0) & (dist < WINDOW) & ((k_pos >= bq) | jnp.logical_not(first))
    lane = lax.broadcasted_iota(jnp.int32, (bq, pair_w), 1)
    lo_q = lane < head_dim
    lane_kv = lax.broadcasted_iota(jnp.int32, (2 * bq, pair_w), 1)
    lo_kv = lane_kv < head_dim
    scale = head_dim ** -0.5
    zero_q = jnp.zeros((bq, pair_w), q_ref.dtype)

    for kv_pair in range(n_kv // 2):
        cols = slice(kv_pair * pair_w, (kv_pair + 1) * pair_w)
        k_two = jnp.concatenate([k_prev[:, cols], k_cur[:, cols]], axis=0)
        v_cols = slice(d_kv + kv_pair * pair_w, d_kv + (kv_pair + 1) * pair_w)
        v_two = jnp.concatenate([k_prev[:, v_cols], k_cur[:, v_cols]], axis=0)
        k_swapped = _swap_halves(k_two)
        v_swapped = _swap_halves(v_two)
        for parity in range(2):
            hk = 2 * kv_pair + parity
            if parity == 0:
                k_dup = jnp.where(lo_kv, k_two, k_swapped)
                v_dup = jnp.where(lo_kv, v_two, v_swapped)
            else:
                k_dup = jnp.where(lo_kv, k_swapped, k_two)
                v_dup = jnp.where(lo_kv, v_swapped, v_two)
            k_dup = k_dup * jnp.asarray(scale, k_dup.dtype)
            q_rows = []
            for p in range(pairs_per_group):
                pair = hk * pairs_per_group + p
                qp = q_ref[:, pair * pair_w:(pair + 1) * pair_w]
                q_rows.append(jnp.where(lo_q, qp, zero_q))
                q_rows.append(jnp.where(lo_q, zero_q, qp))
            q_stack = jnp.concatenate(q_rows, axis=0)
            s_all = lax.dot_general(q_stack, k_dup, (((1,), (1,)), ((), ())),
                                    preferred_element_type=jnp.float32)
            p_rows, inv_rows = [], []
            for g in range(group):
                sink = sink_ref[hk * group + g]
                sc = jnp.where(valid, s_all[g * bq:(g + 1) * bq, :], -jnp.inf)
                m = jnp.maximum(jnp.max(sc, axis=-1, keepdims=True), sink)
                e = jnp.exp(sc - m)
                denom = jnp.sum(e, axis=-1, keepdims=True) + jnp.exp(sink - m)
                p_rows.append(e.astype(v_dup.dtype))
                inv_rows.append(1.0 / denom)
            p_stack = jnp.concatenate(p_rows, axis=0)
            o_all = jnp.dot(p_stack, v_dup, preferred_element_type=jnp.float32)
            for p in range(pairs_per_group):
                pair = hk * pairs_per_group + p
                o_even = o_all[(2 * p) * bq:(2 * p + 1) * bq, :] * inv_rows[2 * p]
                o_odd = o_all[(2 * p + 1) * bq:(2 * p + 2) * bq, :] * inv_rows[2 * p + 1]
                o_pair = jnp.where(lo_q, o_even, o_odd)
                c0 = pair * pair_w
                o_ref[:, c0:c0 + pair_w] = (o_pair * sg_ref[:, c0:c0 + pair_w]).astype(o_ref.dtype)


def _attn_kernel(sink_ref, q_ref, kv_ref, kvp_ref, sg_ref, cast_src, o_ref, cast_dst, *,
                 n_kv, group, head_dim):
    _run_cast(cast_src, cast_dst)
    bq = BLOCK_Q
    n = pl.program_id(0)
    for b in range(q_ref.shape[0] // bq):
        rows = pl.ds(b * bq, bq)
        k_prev = kvp_ref if b == 0 else kv_ref.at[pl.ds((b - 1) * bq, bq), :]
        first = (n == 0) if b == 0 else jnp.bool_(False)
        _attn_block(first, sink_ref, q_ref.at[rows, :], k_prev, kv_ref.at[rows, :],
                    sg_ref.at[rows, :], o_ref.at[rows, :], n_kv=n_kv, group=group,
                    head_dim=head_dim)


def _attention(q, kv, sg_attn, sinks, cast, *, n_kv, head_dim):
    s, d_attn = q.shape
    n_q = d_attn // head_dim
    group = n_q // n_kv
    assert head_dim * 2 == LANES and group % 2 == 0 and n_kv % 2 == 0
    bq = BLOCK_Q
    rows = ATTN_BLOCKS_PER_STEP * bq
    d_kv2 = kv.shape[1]
    grid = (s // rows, 1)
    c_in, c_out, c_shape, c_bytes = _cast_specs(cast, grid)
    block_bytes = 2 * rows * (d_attn * (2 + 4 + 2) + d_kv2 * 2) + 2 * bq * d_kv2 * 2 + c_bytes
    seq = pl.BlockSpec((rows, d_attn), lambda n, _: (n, 0))
    return pl.pallas_call(
        functools.partial(_attn_kernel, n_kv=n_kv, group=group, head_dim=head_dim),
        out_shape=(jax.ShapeDtypeStruct((s, d_attn), jnp.bfloat16), c_shape),
        grid=grid,
        in_specs=[pl.BlockSpec(memory_space=pltpu.SMEM),
                  seq,
                  pl.BlockSpec((rows, d_kv2), lambda n, _: (n, 0)),
                  pl.BlockSpec((bq, d_kv2),
                               lambda n, _: (jnp.maximum(n * ATTN_BLOCKS_PER_STEP - 1, 0), 0)),
                  seq, c_in],
        out_specs=(seq, c_out),
        compiler_params=pltpu.CompilerParams(
            dimension_semantics=("arbitrary", "arbitrary"),
            vmem_limit_bytes=_vmem_limit(block_bytes + 16 * 1024 * 1024)),
        name="swa_attn",
    )(sinks, q, kv, kv, sg_attn, cast.src)


def kernel(x, c, w_ada, b_ada, norm_gain, w_in, conv_w, conv_b, w_gate_a, b_gate_a, w_gate_x,
           b_gate_x, lru_lambda, attn_sinks, w_rnn_proj, w_attn_proj, w_out, final_norm_gain):
    bsz, seq, d = x.shape
    depth = w_in.shape[0]
    d_rnn = w_rnn_proj.shape[1]
    d_attn = w_attn_proj.shape[1]
    n_q = attn_sinks.shape[1]
    head_dim = d_attn // n_q
    d_kv = (w_in.shape[2] - 2 * d_rnn - 2 * d_attn - 2 * d) // 2
    n_kv = d_kv // head_dim
    assert bsz == 1

    xs = x.reshape(bsz * seq, d)
    bf = jnp.bfloat16
    for l in range(depth):
        mod = _modulation(c, w_ada[l], b_ada[l])
        shift, scale, gate = mod[:, :d], mod[:, d:2 * d], mod[:, 2 * d:]
        wl = w_in[l]
        c_u, c_g, c_q, c_kv = 0, d_rnn, 2 * d_rnn, 2 * d_rnn + d_attn
        c_ga = c_kv + 2 * d_kv
        c_mr, c_ma = c_ga + d_attn, c_ga + d_attn + d
        w_kv = _cast_cols(CastJob(wl, c_kv, 2 * d_kv))
        h, kv, w_u = _norm_proj(xs, norm_gain[l].reshape(1, d), scale, shift, w_kv,
                                CastJob(wl, c_u, d_rnn))
        proj = functools.partial(_matmul, h)
        u_rnn, w_g = proj(w_u, out_dtype=jnp.float32, cast=CastJob(wl, c_g, d_rnn),
                          name="proj_u")
        sg_rnn, w_q = proj(w_g, out_dtype=jnp.float32, epilogue=_ep_silu,
                           cast=CastJob(wl, c_q, d_attn), name="proj_g_rnn")
        q, w_ga = proj(w_q, out_dtype=bf, cast=CastJob(wl, c_ga, d_attn), name="proj_q")
        sg_attn, w_mr = proj(w_ga, out_dtype=jnp.float32, epilogue=_ep_silu,
                             cast=CastJob(wl, c_mr, d), name="proj_g_attn")
        sm_rnn, w_ma = proj(w_mr, out_dtype=jnp.float32, epilogue=_ep_sigmoid,
                            cast=CastJob(wl, c_ma, d), name="proj_m_rnn")
        sm_attn, w_rp = proj(w_ma, out_dtype=jnp.float32, epilogue=_ep_sigmoid,
                             cast=CastJob(w_rnn_proj[l], 0, d), name="proj_m_attn")

        y_rnn, w_ap = _rglru(u_rnn, sg_rnn, conv_w[l], conv_b[l], w_gate_a[l], b_gate_a[l],
                             w_gate_x[l], b_gate_x[l], lru_lambda[l],
                             CastJob(w_attn_proj[l], 0, d))
        y_attn, w_o = _attention(q, kv, sg_attn, attn_sinks[l], CastJob(w_out[l], 0, d),
                                 n_kv=n_kv, head_dim=head_dim)

        merged = _merge(y_rnn, y_attn, w_rp, w_ap, sm_rnn, sm_attn)
        if l + 1 < depth:
            xs = _matmul(merged, w_o, out_dtype=jnp.float32, epilogue=_ep_residual,
                         tiles=(xs,), rows=(gate,), name="out_proj")
        else:
            xs = _out_norm(merged, w_o, xs, gate, final_norm_gain.reshape(1, d))
    return xs.reshape(bsz, seq, d)
```

```python
import functools
from typing import NamedTuple

import jax
import jax.numpy as jnp
from jax import lax
from jax.experimental import pallas as pl
from jax.experimental.pallas import tpu as pltpu

EPS = 1e-6
LRU_C = 8.0
WINDOW = 128
BLOCK_Q = 128
ATTN_BLOCKS_PER_STEP = 2

V7X_VMEM_BYTES = 64 * 1024 * 1024
VMEM_HEADROOM_BYTES = 6 * 1024 * 1024
LANES = 128
SUBLANES = 8


def _vmem_limit(block_bytes):
    need = int(block_bytes) + VMEM_HEADROOM_BYTES
    assert need <= V7X_VMEM_BYTES - 2 * 1024 * 1024, need
    return need


def _sigmoid(v):
    return 0.5 * jnp.tanh(0.5 * v) + 0.5


def _silu(v):
    return v * _sigmoid(v)


def _mod_kernel(c_ref, w_ref, b_ref, o_ref):
    c_act = _silu(c_ref[...])
    o_ref[...] = jnp.sum(w_ref[...] * c_act, axis=0, keepdims=True) + b_ref[...]


def _modulation(c, w_ada, b_ada):
    d, n = w_ada.shape
    tn = 512
    return pl.pallas_call(
        _mod_kernel,
        out_shape=jax.ShapeDtypeStruct((1, n), jnp.float32),
        grid=(n // tn,),
        in_specs=[pl.BlockSpec((d, 1), lambda j: (0, 0)),
                  pl.BlockSpec((d, tn), lambda j: (0, j)),
                  pl.BlockSpec((1, tn), lambda j: (0, j))],
        out_specs=pl.BlockSpec((1, tn), lambda j: (0, j)),
        compiler_params=pltpu.CompilerParams(
            dimension_semantics=("arbitrary",),
            vmem_limit_bytes=_vmem_limit(2 * d * tn * 4 + 2 * d * LANES * 4 + d * tn * 4)),
        name="adaln_mod",
    )(c.reshape(d, 1), w_ada, b_ada.reshape(1, n))


def _norm_proj_kernel(x_ref, gain_ref, scale_ref, shift_ref, w_ref, cast_src,
                      h_ref, o_ref, cast_dst):
    _run_cast(cast_src, cast_dst)
    x = x_ref[...]
    ms = jnp.mean(x * x, axis=-1, keepdims=True)
    g = gain_ref[...] * (1.0 + scale_ref[...])
    h = (x * lax.rsqrt(ms + EPS) * g + shift_ref[...]).astype(h_ref.dtype)
    h_ref[...] = h
    o_ref[...] = jnp.dot(h, w_ref[...], preferred_element_type=jnp.float32).astype(o_ref.dtype)


def _norm_proj(x2, gain, scale, shift, w, cast):
    s, d = x2.shape
    n = w.shape[1]
    tm = 512
    grid = (s // tm, 1)
    c_in, c_out, c_shape, c_bytes = _cast_specs(cast, grid)
    row = pl.BlockSpec((1, d), lambda i, _: (0, 0))
    seq = pl.BlockSpec((tm, d), lambda i, _: (i, 0))
    block_bytes = (2 * tm * d * (4 + 2) + tm * d * 4 + d * n * 2 + 2 * tm * n * 2
                   + tm * n * 4 + c_bytes)
    return pl.pallas_call(
        _norm_proj_kernel,
        out_shape=(jax.ShapeDtypeStruct((s, d), jnp.bfloat16),
                   jax.ShapeDtypeStruct((s, n), jnp.bfloat16), c_shape),
        grid=grid,
        in_specs=[seq, row, row, row,
                  pl.BlockSpec((d, n), lambda i, _: (0, 0), pipeline_mode=pl.Buffered(1)),
                  c_in],
        out_specs=(seq, pl.BlockSpec((tm, n), lambda i, _: (i, 0)), c_out),
        compiler_params=pltpu.CompilerParams(
            dimension_semantics=("arbitrary", "arbitrary"),
            vmem_limit_bytes=_vmem_limit(block_bytes)),
        name="norm_proj_kv",
    )(x2, gain, scale, shift, w, cast.src)


CAST_BLOCK_COLS = 1024


class CastJob(NamedTuple):
    src: jax.Array
    col0: int
    cols: int


def _cast_specs(job, grid):
    g0, g1 = grid
    n_steps = g0 * g1
    rows = job.src.shape[0]
    cw = min(job.cols, CAST_BLOCK_COLS)
    n_cb = job.cols // cw
    rb = rows * n_cb // n_steps
    assert rb * n_steps == rows * n_cb and rb % 16 == 0 and job.col0 % cw == 0
    col_blk0 = job.col0 // cw

    def blk(i, j):
        s = i * g1 + j
        return s // n_cb, s % n_cb

    in_spec = pl.BlockSpec((rb, cw), lambda i, j: (blk(i, j)[0], col_blk0 + blk(i, j)[1]))
    out_spec = pl.BlockSpec((rb, cw), lambda i, j: blk(i, j))
    out_shape = jax.ShapeDtypeStruct((rows, job.cols), jnp.bfloat16)
    return in_spec, out_spec, out_shape, 2 * rb * cw * (4 + 2)


def _run_cast(src_ref, dst_ref):
    dst_ref[...] = src_ref[...].astype(dst_ref.dtype)


def _cast_cols(job):
    rows = job.src.shape[0]
    rb = 512
    cw = min(job.cols, CAST_BLOCK_COLS)
    assert rows % rb == 0 and job.cols % cw == 0 and job.col0 % cw == 0
    col_blk0 = job.col0 // cw
    return pl.pallas_call(
        _run_cast,
        out_shape=jax.ShapeDtypeStruct((rows, job.cols), jnp.bfloat16),
        grid=(rows // rb, job.cols // cw),
        in_specs=[pl.BlockSpec((rb, cw), lambda i, j: (i, col_blk0 + j))],
        out_specs=pl.BlockSpec((rb, cw), lambda i, j: (i, j)),
        compiler_params=pltpu.CompilerParams(
            dimension_semantics=("arbitrary", "arbitrary"),
            vmem_limit_bytes=_vmem_limit(2 * rb * cw * (4 + 2))),
        name="cast_cols",
    )(job.src)


def _ep_identity(acc):
    return acc


def _ep_silu(acc):
    return _silu(acc)


def _ep_sigmoid(acc):
    return _sigmoid(acc)


def _ep_residual(acc, x, gate):
    return x + gate * acc


def _matmul_kernel(*refs, epilogue, n_tiles, n_rows, has_cast):
    lhs_ref, rhs_ref = refs[0], refs[1]
    n_in = 2 + n_tiles + n_rows
    tile_refs = refs[2:2 + n_tiles]
    row_refs = refs[2 + n_tiles:n_in]
    o_ref = refs[n_in + has_cast]
    if has_cast:
        _run_cast(refs[n_in], refs[n_in + 2])
    acc = jnp.dot(lhs_ref[...], rhs_ref[...], preferred_element_type=jnp.float32)
    extras = [r[...] for r in tile_refs] + [r[...] for r in row_refs]
    o_ref[...] = epilogue(acc, *extras).astype(o_ref.dtype)


def _matmul(lhs, rhs, *, out_dtype, epilogue=_ep_identity, tiles=(), rows=(), tn=1024,
            cast=None, name):
    m, k = lhs.shape
    n_cols = rhs.shape[1]
    tm = 512 if tiles else 1024
    assert rhs.shape[0] == k and n_cols % tn == 0 and m % tm == 0
    grid = (m // tm, n_cols // tn)
    in_specs = [pl.BlockSpec((tm, k), lambda i, j: (i, 0)),
                pl.BlockSpec((k, tn), lambda i, j: (0, j))]
    in_specs += [pl.BlockSpec((tm, tn), lambda i, j: (i, j)) for _ in tiles]
    in_specs += [pl.BlockSpec((1, tn), lambda i, j: (0, j)) for _ in rows]
    out_bytes = jnp.dtype(out_dtype).itemsize
    tile_bytes = sum(t.dtype.itemsize for t in tiles)
    block_bytes = 2 * (tm * k * lhs.dtype.itemsize + k * tn * rhs.dtype.itemsize
                       + tm * tn * (out_bytes + tile_bytes)) + 2 * tm * tn * 4
    out_shape = jax.ShapeDtypeStruct((m, n_cols), out_dtype)
    out_spec = pl.BlockSpec((tm, tn), lambda i, j: (i, j))
    args = (lhs, rhs, *tiles, *rows)
    if cast is not None:
        c_in, c_out, c_shape, c_bytes = _cast_specs(cast, grid)
        in_specs.append(c_in)
        out_shape, out_spec = (out_shape, c_shape), (out_spec, c_out)
        args += (cast.src,)
        block_bytes += c_bytes
    return pl.pallas_call(
        functools.partial(_matmul_kernel, epilogue=epilogue, n_tiles=len(tiles),
                          n_rows=len(rows), has_cast=cast is not None),
        out_shape=out_shape,
        grid=grid,
        in_specs=in_specs,
        out_specs=out_spec,
        compiler_params=pltpu.CompilerParams(
            dimension_semantics=("arbitrary", "arbitrary"),
            vmem_limit_bytes=_vmem_limit(block_bytes)),
        name=name,
    )(*args)


def _merge_kernel(yr_ref, ya_ref, wr_ref, wa_ref, smr_ref, sma_ref, o_ref):
    p_rnn = jnp.dot(yr_ref[...], wr_ref[...], preferred_element_type=jnp.float32)
    p_attn = jnp.dot(ya_ref[...], wa_ref[...], preferred_element_type=jnp.float32)
    o_ref[...] = (smr_ref[...] * p_rnn + sma_ref[...] * p_attn).astype(o_ref.dtype)


def _merge(y_rnn, y_attn, w_rnn, w_attn, sm_rnn, sm_attn):
    m, k = y_rnn.shape
    n = w_rnn.shape[1]
    tm, tn = 512, 1024
    lhs = pl.BlockSpec((tm, k), lambda j, i: (i, 0))
    rhs = pl.BlockSpec((k, tn), lambda j, i: (0, j), pipeline_mode=pl.Buffered(1))
    tile = pl.BlockSpec((tm, tn), lambda j, i: (i, j))
    block_bytes = (2 * (2 * tm * k * 2 + tm * tn * (4 + 4 + 2)) + 2 * k * tn * 2
                   + 2 * tm * tn * 4)
    return pl.pallas_call(
        _merge_kernel,
        out_shape=jax.ShapeDtypeStruct((m, n), jnp.bfloat16),
        grid=(n // tn, m // tm),
        in_specs=[lhs, lhs, rhs, rhs, tile, tile],
        out_specs=tile,
        compiler_params=pltpu.CompilerParams(
            dimension_semantics=("arbitrary", "arbitrary"),
            vmem_limit_bytes=_vmem_limit(block_bytes)),
        name="merge_proj",
    )(y_rnn, y_attn, w_rnn, w_attn, sm_rnn, sm_attn)


def _out_norm_kernel(m_ref, w_ref, x_ref, gate_ref, gain_ref, o_ref, res_ref):
    j = pl.program_id(1)
    n_j = pl.num_programs(1)
    tn = w_ref.shape[1]
    acc = jnp.dot(m_ref[...], w_ref[...], preferred_element_type=jnp.float32)
    res_ref[j] = x_ref[...] + gate_ref[...] * acc

    @pl.when(j == n_j - 1)
    def _():
        n_tiles = res_ref.shape[0]
        ss = jnp.sum(res_ref[0] * res_ref[0], axis=-1, keepdims=True)
        for t in range(1, n_tiles):
            ss = ss + jnp.sum(res_ref[t] * res_ref[t], axis=-1, keepdims=True)
        inv = lax.rsqrt(ss / (n_tiles * tn) + EPS)
        for t in range(n_tiles):
            o_ref[:, t * tn:(t + 1) * tn] = res_ref[t] * inv * gain_ref[:, t * tn:(t + 1) * tn]


def _out_norm(merged, w, x2, gate, gain):
    m, k = merged.shape
    n = w.shape[1]
    tm, tn = 512, 1024
    block_bytes = (2 * (tm * k * 2 + k * tn * 2 + tm * tn * 4 + tm * n * 4) + tm * n * 4
                   + tm * tn * 4)
    return pl.pallas_call(
        _out_norm_kernel,
        out_shape=jax.ShapeDtypeStruct((m, n), jnp.float32),
        grid=(m // tm, n // tn),
        in_specs=[pl.BlockSpec((tm, k), lambda i, j: (i, 0)),
                  pl.BlockSpec((k, tn), lambda i, j: (0, j)),
                  pl.BlockSpec((tm, tn), lambda i, j: (i, j)),
                  pl.BlockSpec((1, tn), lambda i, j: (0, j)),
                  pl.BlockSpec((1, n), lambda i, j: (0, 0))],
        out_specs=pl.BlockSpec((tm, n), lambda i, j: (i, 0)),
        scratch_shapes=[pltpu.VMEM((n // tn, tm, tn), jnp.float32)],
        compiler_params=pltpu.CompilerParams(
            dimension_semantics=("arbitrary", "arbitrary"),
            vmem_limit_bytes=_vmem_limit(block_bytes)),
        name="out_norm",
    )(merged, w, x2, gate, gain)


RNN_SUB_ROWS = 512
RNN_BLOCK_ROWS = 4096
RNN_HALO_GROUPS = 4
SQRT_FLOOR = 1e-30


def _slabs(ref, rows, n_slabs):
    return jnp.concatenate([ref[l, rows, :] for l in range(n_slabs)], axis=1)


def _rglru_kernel(u_ref, sg_ref, cw_ref, cb_ref, wa_ref, ba_ref, wx_ref, bx_ref, lam_ref,
                  cast_src, y_ref, cast_dst, uperm, hbuf, pbuf, carry_ref, *, conv_width):
    ts = RNN_SUB_ROWS
    tc = ts // SUBLANES
    width = u_ref.shape[1]
    n_slabs = width // LANES
    data0 = RNN_HALO_GROUPS * SUBLANES
    taps = conv_width - 1
    assert taps < RNN_HALO_GROUPS and taps <= tc
    _run_cast(cast_src, cast_dst)

    @pl.when(pl.program_id(1) == 0)
    def _():
        uperm[...] = jnp.zeros_like(uperm)
        carry_ref[...] = jnp.zeros_like(carry_ref)

    neg = -lam_ref[...]
    softplus_neg = jnp.maximum(neg, 0.0) + jnp.log1p(jnp.exp(-jnp.abs(neg)))
    half_decay = (-0.5 * LRU_C) * softplus_neg
    half_ba = 0.5 * ba_ref[...]
    half_bx = 0.5 * bx_ref[...]
    cw = cw_ref[...]
    sub = lax.broadcasted_iota(jnp.int32, (SUBLANES, width), 0)

    def group(j):
        return pl.ds(data0 + j * SUBLANES, SUBLANES)

    def sub_tile(s, _):
        r0 = pl.multiple_of(s * ts, ts)
        prev = [_slabs(uperm, group(tc - k), n_slabs) for k in range(1, taps + 1)]
        for c in range(SUBLANES):
            for l in range(n_slabs):
                uperm[l, pl.ds(data0 + c, tc, stride=SUBLANES), :] = (
                    u_ref[pl.ds(r0 + c * tc, tc), l * LANES:(l + 1) * LANES])
        for k in range(1, taps + 1):
            cur = _slabs(uperm, group(tc - k), n_slabs)
            halo = jnp.where(sub == 0, pltpu.roll(prev[k - 1], 1, axis=0),
                             pltpu.roll(cur, 1, axis=0))
            for l in range(n_slabs):
                uperm[l, group(-k), :] = halo[:, l * LANES:(l + 1) * LANES]
        u = cb_ref[...]
        for k in range(conv_width):
            first = data0 - (taps - k) * SUBLANES
            u = u + cw[k:k + 1, :] * _slabs(uperm, pl.ds(first, ts), n_slabs)
        ub = u.astype(jnp.bfloat16)
        t_r = jnp.tanh(jnp.dot(ub, wa_ref[...], preferred_element_type=jnp.float32) + half_ba)
        t_i = jnp.tanh(jnp.dot(ub, wx_ref[...], preferred_element_type=jnp.float32) + half_bx)
        log_a = half_decay * t_r + half_decay
        a = jnp.exp(log_a)
        half_u = 0.5 * u
        gated_u = half_u * t_i + half_u
        one_m_a2 = 1.0 - a * a
        b = (one_m_a2 * lax.rsqrt(jnp.maximum(one_m_a2, SQRT_FLOOR))) * gated_u

        h_loc = b[0:SUBLANES]
        p_cum = a[0:SUBLANES]
        for j in range(tc):
            if j > 0:
                aj = a[j * SUBLANES:(j + 1) * SUBLANES]
                h_loc = aj * h_loc + b[j * SUBLANES:(j + 1) * SUBLANES]
                p_cum = aj * p_cum
            pbuf[j * SUBLANES:(j + 1) * SUBLANES, :] = p_cum
            for l in range(n_slabs):
                hbuf[l, j * SUBLANES:(j + 1) * SUBLANES, :] = h_loc[:, l * LANES:(l + 1) * LANES]

        a_inc, b_inc = p_cum, h_loc
        shift = 1
        while shift < SUBLANES:
            keep = sub >= shift
            a_prev = jnp.where(keep, pltpu.roll(a_inc, shift, axis=0), 1.0)
            b_prev = jnp.where(keep, pltpu.roll(b_inc, shift, axis=0), 0.0)
            b_inc = a_inc * b_prev + b_inc
            a_inc = a_inc * a_prev
            shift *= 2
        h_in = jnp.broadcast_to(carry_ref[...], (SUBLANES, width))
        chunk_end = b_inc + a_inc * h_in
        entry = jnp.where(sub == 0, h_in, pltpu.roll(chunk_end, 1, axis=0))
        carry_ref[...] = chunk_end[SUBLANES - 1:SUBLANES, :]

        for j in range(tc):
            rows = slice(j * SUBLANES, (j + 1) * SUBLANES)
            h = _slabs(hbuf, rows, n_slabs) + pbuf[rows, :] * entry
            for l in range(n_slabs):
                hbuf[l, rows, :] = h[:, l * LANES:(l + 1) * LANES]
        for c in range(SUBLANES):
            rows = pl.ds(r0 + c * tc, tc)
            h_nat = _slabs(hbuf, pl.ds(c, tc, stride=SUBLANES), n_slabs)
            y_ref[rows, :] = (h_nat * sg_ref[rows, :]).astype(y_ref.dtype)
        return 0

    lax.fori_loop(0, u_ref.shape[0] // ts, sub_tile, 0, unroll=2)


def _rglru(u_rnn, sg_rnn, conv_w, conv_b, w_gate_a, b_gate_a, w_gate_x, b_gate_x, lam, cast):
    s, d = u_rnn.shape
    nb, bw, _ = w_gate_a.shape
    conv_width = conv_w.shape[0]
    tb = RNN_BLOCK_ROWS
    seq = pl.BlockSpec((tb, bw), lambda c, t: (t, c))
    vec = pl.BlockSpec((1, bw), lambda c, t: (0, c))
    gate_w = pl.BlockSpec((None, bw, bw), lambda c, t: (c, 0, 0))
    gate_b = pl.BlockSpec((None, 1, bw), lambda c, t: (c, 0, 0))
    block_bytes = (2 * tb * bw * (4 + 4 + 2) + 4 * bw * bw * 2
                   + 3 * (RNN_SUB_ROWS + RNN_HALO_GROUPS * SUBLANES) * bw * 4)
    grid = (nb, s // tb)
    c_in, c_out, c_shape, c_bytes = _cast_specs(cast, grid)
    return pl.pallas_call(
        functools.partial(_rglru_kernel, conv_width=conv_width),
        out_shape=(jax.ShapeDtypeStruct((s, d), jnp.bfloat16), c_shape),
        grid=grid,
        in_specs=[seq, seq,
                  pl.BlockSpec((conv_width, bw), lambda c, t: (0, c)), vec,
                  gate_w, gate_b, gate_w, gate_b, vec, c_in],
        out_specs=(seq, c_out),
        scratch_shapes=[
            pltpu.VMEM((bw // LANES, RNN_SUB_ROWS + RNN_HALO_GROUPS * SUBLANES, LANES),
                       jnp.float32),
            pltpu.VMEM((bw // LANES, RNN_SUB_ROWS, LANES), jnp.float32),
            pltpu.VMEM((RNN_SUB_ROWS, bw), jnp.float32),
            pltpu.VMEM((1, bw), jnp.float32)],
        compiler_params=pltpu.CompilerParams(
            dimension_semantics=("arbitrary", "arbitrary"),
            vmem_limit_bytes=_vmem_limit(block_bytes + c_bytes + 16 * 1024 * 1024)),
        name="rglru",
    )(u_rnn, sg_rnn, conv_w, conv_b.reshape(1, d),
      (0.5 * w_gate_a).astype(jnp.bfloat16), b_gate_a.reshape(nb, 1, bw),
      (0.5 * w_gate_x).astype(jnp.bfloat16), b_gate_x.reshape(nb, 1, bw), lam.reshape(1, d),
      cast.src)


def _swap_halves(v):
    half = v.shape[1] // 2
    return jnp.concatenate([v[:, half:], v[:, :half]], axis=1)


def _attn_block(first, sink_ref, q_ref, k_prev, k_cur, sg_ref, o_ref, *, n_kv, group, head_dim):
    bq = q_ref.shape[0]
    pair_w = 2 * head_dim
    pairs_per_group = group // 2
    d_kv = n_kv * head_dim

    q_pos = lax.broadcasted_iota(jnp.int32, (bq, 2 * bq), 0)
    k_pos = lax.broadcasted_iota(jnp.int32, (bq, 2 * bq), 1)
    dist = q_pos + bq - k_pos
    valid = (dist >= 0) & (dist < WINDOW) & ((k_pos >= bq) | jnp.logical_not(first))
    lane = lax.broadcasted_iota(jnp.int32, (bq, pair_w), 1)
    lo_q = lane < head_dim
    lane_kv = lax.broadcasted_iota(jnp.int32, (2 * bq, pair_w), 1)
    lo_kv = lane_kv < head_dim
    scale = head_dim ** -0.5
    zero_q = jnp.zeros((bq, pair_w), q_ref.dtype)

    for kv_pair in range(n_kv // 2):
        cols = slice(kv_pair * pair_w, (kv_pair + 1) * pair_w)
        k_two = jnp.concatenate([k_prev[:, cols], k_cur[:, cols]], axis=0)
        v_cols = slice(d_kv + kv_pair * pair_w, d_kv + (kv_pair + 1) * pair_w)
        v_two = jnp.concatenate([k_prev[:, v_cols], k_cur[:, v_cols]], axis=0)
        k_swapped = _swap_halves(k_two)
        v_swapped = _swap_halves(v_two)
        for parity in range(2):
            hk = 2 * kv_pair + parity
            if parity == 0:
                k_dup = jnp.where(lo_kv, k_two, k_swapped)
                v_dup = jnp.where(lo_kv, v_two, v_swapped)
            else:
                k_dup = jnp.where(lo_kv, k_swapped, k_two)
                v_dup = jnp.where(lo_kv, v_swapped, v_two)
            k_dup = k_dup * jnp.asarray(scale, k_dup.dtype)
            q_rows = []
            for p in range(pairs_per_group):
                pair = hk * pairs_per_group + p
                qp = q_ref[:, pair * pair_w:(pair + 1) * pair_w]
                q_rows.append(jnp.where(lo_q, qp, zero_q))
                q_rows.append(jnp.where(lo_q, zero_q, qp))
            q_stack = jnp.concatenate(q_rows, axis=0)
            s_all = lax.dot_general(q_stack, k_dup, (((1,), (1,)), ((), ())),
                                    preferred_element_type=jnp.float32)
            p_rows, inv_rows = [], []
            for g in range(group):
                sink = sink_ref[hk * group + g]
                sc = jnp.where(valid, s_all[g * bq:(g + 1) * bq, :], -jnp.inf)
                m = jnp.maximum(jnp.max(sc, axis=-1, keepdims=True), sink)
                e = jnp.exp(sc - m)
                denom = jnp.sum(e, axis=-1, keepdims=True) + jnp.exp(sink - m)
                p_rows.append(e.astype(v_dup.dtype))
                inv_rows.append(1.0 / denom)
            p_stack = jnp.concatenate(p_rows, axis=0)
            o_all = jnp.dot(p_stack, v_dup, preferred_element_type=jnp.float32)
            for p in range(pairs_per_group):
                pair = hk * pairs_per_group + p
                o_even = o_all[(2 * p) * bq:(2 * p + 1) * bq, :] * inv_rows[2 * p]
                o_odd = o_all[(2 * p + 1) * bq:(2 * p + 2) * bq, :] * inv_rows[2 * p + 1]
                o_pair = jnp.where(lo_q, o_even, o_odd)
                c0 = pair * pair_w
                o_ref[:, c0:c0 + pair_w] = (o_pair * sg_ref[:, c0:c0 + pair_w]).astype(o_ref.dtype)


def _attn_kernel(sink_ref, q_ref, kv_ref, kvp_ref, sg_ref, cast_src, o_ref, cast_dst, *,
                 n_kv, group, head_dim):
    _run_cast(cast_src, cast_dst)
    bq = BLOCK_Q
    n = pl.program_id(0)
    for b in range(q_ref.shape[0] // bq):
        rows = pl.ds(b * bq, bq)
        k_prev = kvp_ref if b == 0 else kv_ref.at[pl.ds((b - 1) * bq, bq), :]
        first = (n == 0) if b == 0 else jnp.bool_(False)
        _attn_block(first, sink_ref, q_ref.at[rows, :], k_prev, kv_ref.at[rows, :],
                    sg_ref.at[rows, :], o_ref.at[rows, :], n_kv=n_kv, group=group,
                    head_dim=head_dim)


def _attention(q, kv, sg_attn, sinks, cast, *, n_kv, head_dim):
    s, d_attn = q.shape
    n_q = d_attn // head_dim
    group = n_q // n_kv
    assert head_dim * 2 == LANES and group % 2 == 0 and n_kv % 2 == 0
    bq = BLOCK_Q
    rows = ATTN_BLOCKS_PER_STEP * bq
    d_kv2 = kv.shape[1]
    grid = (s // rows, 1)
    c_in, c_out, c_shape, c_bytes = _cast_specs(cast, grid)
    block_bytes = 2 * rows * (d_attn * (2 + 4 + 2) + d_kv2 * 2) + 2 * bq * d_kv2 * 2 + c_bytes
    seq = pl.BlockSpec((rows, d_attn), lambda n, _: (n, 0))
    return pl.pallas_call(
        functools.partial(_attn_kernel, n_kv=n_kv, group=group, head_dim=head_dim),
        out_shape=(jax.ShapeDtypeStruct((s, d_attn), jnp.bfloat16), c_shape),
        grid=grid,
        in_specs=[pl.BlockSpec(memory_space=pltpu.SMEM),
                  seq,
                  pl.BlockSpec((rows, d_kv2), lambda n, _: (n, 0)),
                  pl.BlockSpec((bq, d_kv2),
                               lambda n, _: (jnp.maximum(n * ATTN_BLOCKS_PER_STEP - 1, 0), 0)),
                  seq, c_in],
        out_specs=(seq, c_out),
        compiler_params=pltpu.CompilerParams(
            dimension_semantics=("arbitrary", "arbitrary"),
            vmem_limit_bytes=_vmem_limit(block_bytes + 16 * 1024 * 1024)),
        name="swa_attn",
    )(sinks, q, kv, kv, sg_attn, cast.src)


def kernel(x, c, w_ada, b_ada, norm_gain, w_in, conv_w, conv_b, w_gate_a, b_gate_a, w_gate_x,
           b_gate_x, lru_lambda, attn_sinks, w_rnn_proj, w_attn_proj, w_out, final_norm_gain):
    bsz, seq, d = x.shape
    depth = w_in.shape[0]
    d_rnn = w_rnn_proj.shape[1]
    d_attn = w_attn_proj.shape[1]
    n_q = attn_sinks.shape[1]
    head_dim = d_attn // n_q
    d_kv = (w_in.shape[2] - 2 * d_rnn - 2 * d_attn - 2 * d) // 2
    n_kv = d_kv // head_dim
    assert bsz == 1

    xs = x.reshape(bsz * seq, d)
    bf = jnp.bfloat16
    for l in range(depth):
        mod = _modulation(c, w_ada[l], b_ada[l])
        shift, scale, gate = mod[:, :d], mod[:, d:2 * d], mod[:, 2 * d:]
        wl = w_in[l]
        c_u, c_g, c_q, c_kv = 0, d_rnn, 2 * d_rnn, 2 * d_rnn + d_attn
        c_ga = c_kv + 2 * d_kv
        c_mr, c_ma = c_ga + d_attn, c_ga + d_attn + d
        w_kv = _cast_cols(CastJob(wl, c_kv, 2 * d_kv))
        h, kv, w_u = _norm_proj(xs, norm_gain[l].reshape(1, d), scale, shift, w_kv,
                                CastJob(wl, c_u, d_rnn))
        proj = functools.partial(_matmul, h)
        u_rnn, w_g = proj(w_u, out_dtype=jnp.float32, cast=CastJob(wl, c_g, d_rnn),
                          name="proj_u")
        sg_rnn, w_q = proj(w_g, out_dtype=jnp.float32, epilogue=_ep_silu,
                           cast=CastJob(wl, c_q, d_attn), name="proj_g_rnn")
        q, w_ga = proj(w_q, out_dtype=bf, cast=CastJob(wl, c_ga, d_attn), name="proj_q")
        sg_attn, w_mr = proj(w_ga, out_dtype=jnp.float32, epilogue=_ep_silu,
                             cast=CastJob(wl, c_mr, d), name="proj_g_attn")
        sm_rnn, w_ma = proj(w_mr, out_dtype=jnp.float32, epilogue=_ep_sigmoid,
                            cast=CastJob(wl, c_ma, d), name="proj_m_rnn")
        sm_attn, w_rp = proj(w_ma, out_dtype=jnp.float32, epilogue=_ep_sigmoid,
                             cast=CastJob(w_rnn_proj[l], 0, d), name="proj_m_attn")

        y_rnn, w_ap = _rglru(u_rnn, sg_rnn, conv_w[l], conv_b[l], w_gate_a[l], b_gate_a[l],
                             w_gate_x[l], b_gate_x[l], lru_lambda[l],
                             CastJob(w_attn_proj[l], 0, d))
        y_attn, w_o = _attention(q, kv, sg_attn, attn_sinks[l], CastJob(w_out[l], 0, d),
                                 n_kv=n_kv, head_dim=head_dim)

        merged = _merge(y_rnn, y_attn, w_rp, w_ap, sm_rnn, sm_attn)
        if l + 1 < depth:
            xs = _matmul(merged, w_o, out_dtype=jnp.float32, epilogue=_ep_residual,
                         tiles=(xs,), rows=(gate,), name="out_proj")
        else:
            xs = _out_norm(merged, w_o, xs, gate, final_norm_gain.reshape(1, d))
    return xs.reshape(bsz, seq, d)
```

```python
import functools
from typing import NamedTuple

import jax
import jax.numpy as jnp
from jax import lax
from jax.experimental import pallas as pl
from jax.experimental.pallas import tpu as pltpu

EPS = 1e-6
LRU_C = 8.0
WINDOW = 128
BLOCK_Q = 128
ATTN_BLOCKS_PER_STEP = 2

V7X_VMEM_BYTES = 64 * 1024 * 1024
VMEM_HEADROOM_BYTES = 6 * 1024 * 1024
LANES = 128
SUBLANES = 8


def _vmem_limit(block_bytes):
    need = int(block_bytes) + VMEM_HEADROOM_BYTES
    assert need <= V7X_VMEM_BYTES - 2 * 1024 * 1024, need
    return need


def _sigmoid(v):
    return 0.5 * jnp.tanh(0.5 * v) + 0.5


def _silu(v):
    return v * _sigmoid(v)


def _mod_kernel(c_ref, w_ref, b_ref, o_ref):
    c_act = _silu(c_ref[...])
    o_ref[...] = jnp.sum(w_ref[...] * c_act, axis=0, keepdims=True) + b_ref[...]


def _modulation(c, w_ada, b_ada):
    d, n = w_ada.shape
    tn = 512
    return pl.pallas_call(
        _mod_kernel,
        out_shape=jax.ShapeDtypeStruct((1, n), jnp.float32),
        grid=(n // tn,),
        in_specs=[pl.BlockSpec((d, 1), lambda j: (0, 0)),
                  pl.BlockSpec((d, tn), lambda j: (0, j)),
                  pl.BlockSpec((1, tn), lambda j: (0, j))],
        out_specs=pl.BlockSpec((1, tn), lambda j: (0, j)),
        compiler_params=pltpu.CompilerParams(
            dimension_semantics=("arbitrary",),
            vmem_limit_bytes=_vmem_limit(2 * d * tn * 4 + 2 * d * LANES * 4 + d * tn * 4)),
        name="adaln_mod",
    )(c.reshape(d, 1), w_ada, b_ada.reshape(1, n))


def _norm_proj_kernel(x_ref, gain_ref, scale_ref, shift_ref, w_ref, cast_src,
                      h_ref, o_ref, cast_dst):
    _run_cast(cast_src, cast_dst)
    x = x_ref[...]
    ms = jnp.mean(x * x, axis=-1, keepdims=True)
    g = gain_ref[...] * (1.0 + scale_ref[...])
    h = (x * lax.rsqrt(ms + EPS) * g + shift_ref[...]).astype(h_ref.dtype)
    h_ref[...] = h
    o_ref[...] = jnp.dot(h, w_ref[...], preferred_element_type=jnp.float32).astype(o_ref.dtype)


def _norm_proj(x2, gain, scale, shift, w, cast):
    s, d = x2.shape
    n = w.shape[1]
    tm = 512
    grid = (s // tm, 1)
    c_in, c_out, c_shape, c_bytes = _cast_specs(cast, grid)
    row = pl.BlockSpec((1, d), lambda i, _: (0, 0))
    seq = pl.BlockSpec((tm, d), lambda i, _: (i, 0))
    block_bytes = (2 * tm * d * (4 + 2) + tm * d * 4 + d * n * 2 + 2 * tm * n * 2
                   + tm * n * 4 + c_bytes)
    return pl.pallas_call(
        _norm_proj_kernel,
        out_shape=(jax.ShapeDtypeStruct((s, d), jnp.bfloat16),
                   jax.ShapeDtypeStruct((s, n), jnp.bfloat16), c_shape),
        grid=grid,
        in_specs=[seq, row, row, row,
                  pl.BlockSpec((d, n), lambda i, _: (0, 0), pipeline_mode=pl.Buffered(1)),
                  c_in],
        out_specs=(seq, pl.BlockSpec((tm, n), lambda i, _: (i, 0)), c_out),
        compiler_params=pltpu.CompilerParams(
            dimension_semantics=("arbitrary", "arbitrary"),
            vmem_limit_bytes=_vmem_limit(block_bytes)),
        name="norm_proj_kv",
    )(x2, gain, scale, shift, w, cast.src)


CAST_BLOCK_COLS = 1024


class CastJob(NamedTuple):
    src: jax.Array
    col0: int
    cols: int


def _cast_specs(job, grid):
    g0, g1 = grid
    n_steps = g0 * g1
    rows = job.src.shape[0]
    cw = min(job.cols, CAST_BLOCK_COLS)
    n_cb = job.cols // cw
    rb = rows * n_cb // n_steps
    assert rb * n_steps == rows * n_cb and rb % 16 == 0 and job.col0 % cw == 0
    col_blk0 = job.col0 // cw

    def blk(i, j):
        s = i * g1 + j
        return s // n_cb, s % n_cb

    in_spec = pl.BlockSpec((rb, cw), lambda i, j: (blk(i, j)[0], col_blk0 + blk(i, j)[1]))
    out_spec = pl.BlockSpec((rb, cw), lambda i, j: blk(i, j))
    out_shape = jax.ShapeDtypeStruct((rows, job.cols), jnp.bfloat16)
    return in_spec, out_spec, out_shape, 2 * rb * cw * (4 + 2)


def _run_cast(src_ref, dst_ref):
    dst_ref[...] = src_ref[...].astype(dst_ref.dtype)


def _cast_cols(job):
    rows = job.src.shape[0]
    rb = 512
    cw = min(job.cols, CAST_BLOCK_COLS)
    assert rows % rb == 0 and job.cols % cw == 0 and job.col0 % cw == 0
    col_blk0 = job.col0 // cw
    return pl.pallas_call(
        _run_cast,
        out_shape=jax.ShapeDtypeStruct((rows, job.cols), jnp.bfloat16),
        grid=(rows // rb, job.cols // cw),
        in_specs=[pl.BlockSpec((rb, cw), lambda i, j: (i, col_blk0 + j))],
        out_specs=pl.BlockSpec((rb, cw), lambda i, j: (i, j)),
        compiler_params=pltpu.CompilerParams(
            dimension_semantics=("arbitrary", "arbitrary"),
            vmem_limit_bytes=_vmem_limit(2 * rb * cw * (4 + 2))),
        name="cast_cols",
    )(job.src)


def _ep_identity(acc):
    return acc


def _ep_silu(acc):
    return _silu(acc)


def _ep_sigmoid(acc):
    return _sigmoid(acc)


def _ep_residual(acc, x, gate):
    return x + gate * acc


def _matmul_kernel(*refs, epilogue, n_tiles, n_rows, has_cast):
    lhs_ref, rhs_ref = refs[0], refs[1]
    n_in = 2 + n_tiles + n_rows
    tile_refs = refs[2:2 + n_tiles]
    row_refs = refs[2 + n_tiles:n_in]
    o_ref = refs[n_in + has_cast]
    if has_cast:
        _run_cast(refs[n_in], refs[n_in + 2])
    acc = jnp.dot(lhs_ref[...], rhs_ref[...], preferred_element_type=jnp.float32)
    extras = [r[...] for r in tile_refs] + [r[...] for r in row_refs]
    o_ref[...] = epilogue(acc, *extras).astype(o_ref.dtype)


def _matmul(lhs, rhs, *, out_dtype, epilogue=_ep_identity, tiles=(), rows=(), tn=1024,
            cast=None, name):
    m, k = lhs.shape
    n_cols = rhs.shape[1]
    tm = 512 if tiles else 1024
    assert rhs.shape[0] == k and n_cols % tn == 0 and m % tm == 0
    grid = (m // tm, n_cols // tn)
    in_specs = [pl.BlockSpec((tm, k), lambda i, j: (i, 0)),
                pl.BlockSpec((k, tn), lambda i, j: (0, j))]
    in_specs += [pl.BlockSpec((tm, tn), lambda i, j: (i, j)) for _ in tiles]
    in_specs += [pl.BlockSpec((1, tn), lambda i, j: (0, j)) for _ in rows]
    out_bytes = jnp.dtype(out_dtype).itemsize
    tile_bytes = sum(t.dtype.itemsize for t in tiles)
    block_bytes = 2 * (tm * k * lhs.dtype.itemsize + k * tn * rhs.dtype.itemsize
                       + tm * tn * (out_bytes + tile_bytes)) + 2 * tm * tn * 4
    out_shape = jax.ShapeDtypeStruct((m, n_cols), out_dtype)
    out_spec = pl.BlockSpec((tm, tn), lambda i, j: (i, j))
    args = (lhs, rhs, *tiles, *rows)
    if cast is not None:
        c_in, c_out, c_shape, c_bytes = _cast_specs(cast, grid)
        in_specs.append(c_in)
        out_shape, out_spec = (out_shape, c_shape), (out_spec, c_out)
        args += (cast.src,)
        block_bytes += c_bytes
    return pl.pallas_call(
        functools.partial(_matmul_kernel, epilogue=epilogue, n_tiles=len(tiles),
                          n_rows=len(rows), has_cast=cast is not None),
        out_shape=out_shape,
        grid=grid,
        in_specs=in_specs,
        out_specs=out_spec,
        compiler_params=pltpu.CompilerParams(
            dimension_semantics=("arbitrary", "arbitrary"),
            vmem_limit_bytes=_vmem_limit(block_bytes)),
        name=name,
    )(*args)


def _merge_kernel(yr_ref, ya_ref, wr_ref, wa_ref, smr_ref, sma_ref, o_ref):
    p_rnn = jnp.dot(yr_ref[...], wr_ref[...], preferred_element_type=jnp.float32)
    p_attn = jnp.dot(ya_ref[...], wa_ref[...], preferred_element_type=jnp.float32)
    o_ref[...] = (smr_ref[...] * p_rnn + sma_ref[...] * p_attn).astype(o_ref.dtype)


def _merge(y_rnn, y_attn, w_rnn, w_attn, sm_rnn, sm_attn):
    m, k = y_rnn.shape
    n = w_rnn.shape[1]
    tm, tn = 512, 1024
    lhs = pl.BlockSpec((tm, k), lambda j, i: (i, 0))
    rhs = pl.BlockSpec((k, tn), lambda j, i: (0, j), pipeline_mode=pl.Buffered(1))
    tile = pl.BlockSpec((tm, tn), lambda j, i: (i, j))
    block_bytes = (2 * (2 * tm * k * 2 + tm * tn * (4 + 4 + 2)) + 2 * k * tn * 2
                   + 2 * tm * tn * 4)
    return pl.pallas_call(
        _merge_kernel,
        out_shape=jax.ShapeDtypeStruct((m, n), jnp.bfloat16),
        grid=(n // tn, m // tm),
        in_specs=[lhs, lhs, rhs, rhs, tile, tile],
        out_specs=tile,
        compiler_params=pltpu.CompilerParams(
            dimension_semantics=("arbitrary", "arbitrary"),
            vmem_limit_bytes=_vmem_limit(block_bytes)),
        name="merge_proj",
    )(y_rnn, y_attn, w_rnn, w_attn, sm_rnn, sm_attn)


def _out_norm_kernel(m_ref, w_ref, x_ref, gate_ref, gain_ref, o_ref):
    j = pl.program_id(1)
    n_j = pl.num_programs(1)
    tn = w_ref.shape[1]
    acc = jnp.dot(m_ref[...], w_ref[...], preferred_element_type=jnp.float32)
    o_ref[:, pl.ds(pl.multiple_of(j * tn, tn), tn)] = x_ref[...] + gate_ref[...] * acc

    @pl.when(j == n_j - 1)
    def _():
        n_tiles = o_ref.shape[1] // tn
        ss = None
        for t in range(n_tiles):
            r = o_ref[:, t * tn:(t + 1) * tn]
            part = jnp.sum(r * r, axis=-1, keepdims=True)
            ss = part if ss is None else ss + part
        inv = lax.rsqrt(ss / (n_tiles * tn) + EPS)
        for t in range(n_tiles):
            cols = slice(t * tn, (t + 1) * tn)
            o_ref[:, cols] = o_ref[:, cols] * inv * gain_ref[:, cols]


def _out_norm(merged, w, x2, gate, gain):
    m, k = merged.shape
    n = w.shape[1]
    tm, tn = 1024, 512
    block_bytes = 2 * (tm * k * 2 + k * tn * 2 + tm * tn * 4) + tm * n * 4 + 3 * tm * tn * 4
    return pl.pallas_call(
        _out_norm_kernel,
        out_shape=jax.ShapeDtypeStruct((m, n), jnp.float32),
        grid=(m // tm, n // tn),
        in_specs=[pl.BlockSpec((tm, k), lambda i, j: (i, 0)),
                  pl.BlockSpec((k, tn), lambda i, j: (0, j)),
                  pl.BlockSpec((tm, tn), lambda i, j: (i, j)),
                  pl.BlockSpec((1, tn), lambda i, j: (0, j)),
                  pl.BlockSpec((1, n), lambda i, j: (0, 0))],
        out_specs=pl.BlockSpec((tm, n), lambda i, j: (i, 0), pipeline_mode=pl.Buffered(1)),
        compiler_params=pltpu.CompilerParams(
            dimension_semantics=("arbitrary", "arbitrary"),
            vmem_limit_bytes=_vmem_limit(block_bytes)),
        name="out_norm",
    )(merged, w, x2, gate, gain)


RNN_SUB_ROWS = 512
RNN_BLOCK_ROWS = 4096
RNN_HALO_GROUPS = 4
SQRT_FLOOR = 1e-30


def _slabs(ref, rows, n_slabs):
    return jnp.concatenate([ref[l, rows, :] for l in range(n_slabs)], axis=1)


def _rglru_kernel(u_ref, sg_ref, cw_ref, cb_ref, wa_ref, ba_ref, wx_ref, bx_ref, lam_ref,
                  cast_src, y_ref, cast_dst, uperm, hbuf, pbuf, carry_ref, *, conv_width):
    ts = RNN_SUB_ROWS
    tc = ts // SUBLANES
    width = u_ref.shape[1]
    n_slabs = width // LANES
    data0 = RNN_HALO_GROUPS * SUBLANES
    taps = conv_width - 1
    assert taps < RNN_HALO_GROUPS and taps <= tc
    _run_cast(cast_src, cast_dst)

    @pl.when(pl.program_id(1) == 0)
    def _():
        uperm[...] = jnp.zeros_like(uperm)
        carry_ref[...] = jnp.zeros_like(carry_ref)

    neg = -lam_ref[...]
    softplus_neg = jnp.maximum(neg, 0.0) + jnp.log1p(jnp.exp(-jnp.abs(neg)))
    half_decay = (-0.5 * LRU_C) * softplus_neg
    half_ba = 0.5 * ba_ref[...]
    half_bx = 0.5 * bx_ref[...]
    cw = cw_ref[...]
    sub = lax.broadcasted_iota(jnp.int32, (SUBLANES, width), 0)

    def group(j):
        return pl.ds(data0 + j * SUBLANES, SUBLANES)

    def sub_tile(s, _):
        r0 = pl.multiple_of(s * ts, ts)
        prev = [_slabs(uperm, group(tc - k), n_slabs) for k in range(1, taps + 1)]
        for c in range(SUBLANES):
            for l in range(n_slabs):
                uperm[l, pl.ds(data0 + c, tc, stride=SUBLANES), :] = (
                    u_ref[pl.ds(r0 + c * tc, tc), l * LANES:(l + 1) * LANES])
        for k in range(1, taps + 1):
            cur = _slabs(uperm, group(tc - k), n_slabs)
            halo = jnp.where(sub == 0, pltpu.roll(prev[k - 1], 1, axis=0),
                             pltpu.roll(cur, 1, axis=0))
            for l in range(n_slabs):
                uperm[l, group(-k), :] = halo[:, l * LANES:(l + 1) * LANES]
        u = cb_ref[...]
        for k in range(conv_width):
            first = data0 - (taps - k) * SUBLANES
            u = u + cw[k:k + 1, :] * _slabs(uperm, pl.ds(first, ts), n_slabs)
        ub = u.astype(jnp.bfloat16)
        t_r = jnp.tanh(jnp.dot(ub, wa_ref[...], preferred_element_type=jnp.float32) + half_ba)
        t_i = jnp.tanh(jnp.dot(ub, wx_ref[...], preferred_element_type=jnp.float32) + half_bx)
        log_a = half_decay * t_r + half_decay
        a = jnp.exp(log_a)
        half_u = 0.5 * u
        gated_u = half_u * t_i + half_u
        one_m_a2 = 1.0 - a * a
        b = (one_m_a2 * lax.rsqrt(jnp.maximum(one_m_a2, SQRT_FLOOR))) * gated_u

        h_loc = b[0:SUBLANES]
        p_cum = a[0:SUBLANES]
        for j in range(tc):
            if j > 0:
                aj = a[j * SUBLANES:(j + 1) * SUBLANES]
                h_loc = aj * h_loc + b[j * SUBLANES:(j + 1) * SUBLANES]
                p_cum = aj * p_cum
            pbuf[j * SUBLANES:(j + 1) * SUBLANES, :] = p_cum
            for l in range(n_slabs):
                hbuf[l, j * SUBLANES:(j + 1) * SUBLANES, :] = h_loc[:, l * LANES:(l + 1) * LANES]

        a_inc, b_inc = p_cum, h_loc
        shift = 1
        while shift < SUBLANES:
            keep = sub >= shift
            a_prev = jnp.where(keep, pltpu.roll(a_inc, shift, axis=0), 1.0)
            b_prev = jnp.where(keep, pltpu.roll(b_inc, shift, axis=0), 0.0)
            b_inc = a_inc * b_prev + b_inc
            a_inc = a_inc * a_prev
            shift *= 2
        h_in = jnp.broadcast_to(carry_ref[...], (SUBLANES, width))
        chunk_end = b_inc + a_inc * h_in
        entry = jnp.where(sub == 0, h_in, pltpu.roll(chunk_end, 1, axis=0))
        carry_ref[...] = chunk_end[SUBLANES - 1:SUBLANES, :]

        for j in range(tc):
            rows = slice(j * SUBLANES, (j + 1) * SUBLANES)
            h = _slabs(hbuf, rows, n_slabs) + pbuf[rows, :] * entry
            for l in range(n_slabs):
                hbuf[l, rows, :] = h[:, l * LANES:(l + 1) * LANES]
        for c in range(SUBLANES):
            rows = pl.ds(r0 + c * tc, tc)
            h_nat = _slabs(hbuf, pl.ds(c, tc, stride=SUBLANES), n_slabs)
            y_ref[rows, :] = (h_nat * sg_ref[rows, :]).astype(y_ref.dtype)
        return 0

    lax.fori_loop(0, u_ref.shape[0] // ts, sub_tile, 0, unroll=2)


def _rglru(u_rnn, sg_rnn, conv_w, conv_b, w_gate_a, b_gate_a, w_gate_x, b_gate_x, lam, cast):
    s, d = u_rnn.shape
    nb, bw, _ = w_gate_a.shape
    conv_width = conv_w.shape[0]
    tb = RNN_BLOCK_ROWS
    seq = pl.BlockSpec((tb, bw), lambda c, t: (t, c))
    vec = pl.BlockSpec((1, bw), lambda c, t: (0, c))
    gate_w = pl.BlockSpec((None, bw, bw), lambda c, t: (c, 0, 0))
    gate_b = pl.BlockSpec((None, 1, bw), lambda c, t: (c, 0, 0))
    block_bytes = (2 * tb * bw * (4 + 4 + 2) + 4 * bw * bw * 2
                   + 3 * (RNN_SUB_ROWS + RNN_HALO_GROUPS * SUBLANES) * bw * 4)
    grid = (nb, s // tb)
    c_in, c_out, c_shape, c_bytes = _cast_specs(cast, grid)
    return pl.pallas_call(
        functools.partial(_rglru_kernel, conv_width=conv_width),
        out_shape=(jax.ShapeDtypeStruct((s, d), jnp.bfloat16), c_shape),
        grid=grid,
        in_specs=[seq, seq,
                  pl.BlockSpec((conv_width, bw), lambda c, t: (0, c)), vec,
                  gate_w, gate_b, gate_w, gate_b, vec, c_in],
        out_specs=(seq, c_out),
        scratch_shapes=[
            pltpu.VMEM((bw // LANES, RNN_SUB_ROWS + RNN_HALO_GROUPS * SUBLANES, LANES),
                       jnp.float32),
            pltpu.VMEM((bw // LANES, RNN_SUB_ROWS, LANES), jnp.float32),
            pltpu.VMEM((RNN_SUB_ROWS, bw), jnp.float32),
            pltpu.VMEM((1, bw), jnp.float32)],
        compiler_params=pltpu.CompilerParams(
            dimension_semantics=("arbitrary", "arbitrary"),
            vmem_limit_bytes=_vmem_limit(block_bytes + c_bytes + 16 * 1024 * 1024)),
        name="rglru",
    )(u_rnn, sg_rnn, conv_w, conv_b.reshape(1, d),
      (0.5 * w_gate_a).astype(jnp.bfloat16), b_gate_a.reshape(nb, 1, bw),
      (0.5 * w_gate_x).astype(jnp.bfloat16), b_gate_x.reshape(nb, 1, bw), lam.reshape(1, d),
      cast.src)


def _swap_halves(v):
    half = v.shape[1] // 2
    return jnp.concatenate([v[:, half:], v[:, :half]], axis=1)


def _attn_block(first, sink_ref, q_ref, k_prev, k_cur, sg_ref, o_ref, *, n_kv, group, head_dim):
    bq = q_ref.shape[0]
    pair_w = 2 * head_dim
    pairs_per_group = group // 2
    d_kv = n_kv * head_dim

    q_pos = lax.broadcasted_iota(jnp.int32, (bq, 2 * bq), 0)
    k_pos = lax.broadcasted_iota(jnp.int32, (bq, 2 * bq), 1)
    dist = q_pos + bq - k_pos
    valid = (dist >= 0) & (dist < WINDOW) & ((k_pos >= bq) | jnp.logical_not(first))
    lane = lax.broadcasted_iota(jnp.int32, (bq, pair_w), 1)
    lo_q = lane < head_dim
    lane_kv = lax.broadcasted_iota(jnp.int32, (2 * bq, pair_w), 1)
    lo_kv = lane_kv < head_dim
    scale = head_dim ** -0.5
    zero_q = jnp.zeros((bq, pair_w), q_ref.dtype)

    for kv_pair in range(n_kv // 2):
        cols = slice(kv_pair * pair_w, (kv_pair + 1) * pair_w)
        k_two = jnp.concatenate([k_prev[:, cols], k_cur[:, cols]], axis=0)
        v_cols = slice(d_kv + kv_pair * pair_w, d_kv + (kv_pair + 1) * pair_w)
        v_two = jnp.concatenate([k_prev[:, v_cols], k_cur[:, v_cols]], axis=0)
        k_swapped = _swap_halves(k_two)
        v_swapped = _swap_halves(v_two)
        for parity in range(2):
            hk = 2 * kv_pair + parity
            if parity == 0:
                k_dup = jnp.where(lo_kv, k_two, k_swapped)
                v_dup = jnp.where(lo_kv, v_two, v_swapped)
            else:
                k_dup = jnp.where(lo_kv, k_swapped, k_two)
                v_dup = jnp.where(lo_kv, v_swapped, v_two)
            k_dup = k_dup * jnp.asarray(scale, k_dup.dtype)
            q_rows = []
            for p in range(pairs_per_group):
                pair = hk * pairs_per_group + p
                qp = q_ref[:, pair * pair_w:(pair + 1) * pair_w]
                q_rows.append(jnp.where(lo_q, qp, zero_q))
                q_rows.append(jnp.where(lo_q, zero_q, qp))
            q_stack = jnp.concatenate(q_rows, axis=0)
            s_all = lax.dot_general(q_stack, k_dup, (((1,), (1,)), ((), ())),
                                    preferred_element_type=jnp.float32)
            p_rows, inv_rows = [], []
            for g in range(group):
                sink = sink_ref[hk * group + g]
                sc = jnp.where(valid, s_all[g * bq:(g + 1) * bq, :], -jnp.inf)
                m = jnp.maximum(jnp.max(sc, axis=-1, keepdims=True), sink)
                e = jnp.exp(sc - m)
                denom = jnp.sum(e, axis=-1, keepdims=True) + jnp.exp(sink - m)
                p_rows.append(e.astype(v_dup.dtype))
                inv_rows.append(1.0 / denom)
            p_stack = jnp.concatenate(p_rows, axis=0)
            o_all = jnp.dot(p_stack, v_dup, preferred_element_type=jnp.float32)
            for p in range(pairs_per_group):
                pair = hk * pairs_per_group + p
                o_even = o_all[(2 * p) * bq:(2 * p + 1) * bq, :] * inv_rows[2 * p]
                o_odd = o_all[(2 * p + 1) * bq:(2 * p + 2) * bq, :] * inv_rows[2 * p + 1]
                o_pair = jnp.where(lo_q, o_even, o_odd)
                c0 = pair * pair_w
                o_ref[:, c0:c0 + pair_w] = (o_pair * sg_ref[:, c0:c0 + pair_w]).astype(o_ref.dtype)


def _attn_kernel(sink_ref, q_ref, kv_ref, kvp_ref, sg_ref, cast_src, o_ref, cast_dst, *,
                 n_kv, group, head_dim):
    _run_cast(cast_src, cast_dst)
    bq = BLOCK_Q
    n = pl.program_id(0)
    for b in range(q_ref.shape[0] // bq):
        rows = pl.ds(b * bq, bq)
        k_prev = kvp_ref if b == 0 else kv_ref.at[pl.ds((b - 1) * bq, bq), :]
        first = (n == 0) if b == 0 else jnp.bool_(False)
        _attn_block(first, sink_ref, q_ref.at[rows, :], k_prev, kv_ref.at[rows, :],
                    sg_ref.at[rows, :], o_ref.at[rows, :], n_kv=n_kv, group=group,
                    head_dim=head_dim)


def _attention(q, kv, sg_attn, sinks, cast, *, n_kv, head_dim):
    s, d_attn = q.shape
    n_q = d_attn // head_dim
    group = n_q // n_kv
    assert head_dim * 2 == LANES and group % 2 == 0 and n_kv % 2 == 0
    bq = BLOCK_Q
    rows = ATTN_BLOCKS_PER_STEP * bq
    d_kv2 = kv.shape[1]
    grid = (s // rows, 1)
    c_in, c_out, c_shape, c_bytes = _cast_specs(cast, grid)
    block_bytes = 2 * rows * (d_attn * (2 + 4 + 2) + d_kv2 * 2) + 2 * bq * d_kv2 * 2 + c_bytes
    seq = pl.BlockSpec((rows, d_attn), lambda n, _: (n, 0))
    return pl.pallas_call(
        functools.partial(_attn_kernel, n_kv=n_kv, group=group, head_dim=head_dim),
        out_shape=(jax.ShapeDtypeStruct((s, d_attn), jnp.bfloat16), c_shape),
        grid=grid,
        in_specs=[pl.BlockSpec(memory_space=pltpu.SMEM),
                  seq,
                  pl.BlockSpec((rows, d_kv2), lambda n, _: (n, 0)),
                  pl.BlockSpec((bq, d_kv2),
                               lambda n, _: (jnp.maximum(n * ATTN_BLOCKS_PER_STEP - 1, 0), 0)),
                  seq, c_in],
        out_specs=(seq, c_out),
        compiler_params=pltpu.CompilerParams(
            dimension_semantics=("arbitrary", "arbitrary"),
            vmem_limit_bytes=_vmem_limit(block_bytes + 16 * 1024 * 1024)),
        name="swa_attn",
    )(sinks, q, kv, kv, sg_attn, cast.src)


def kernel(x, c, w_ada, b_ada, norm_gain, w_in, conv_w, conv_b, w_gate_a, b_gate_a, w_gate_x,
           b_gate_x, lru_lambda, attn_sinks, w_rnn_proj, w_attn_proj, w_out, final_norm_gain):
    bsz, seq, d = x.shape
    depth = w_in.shape[0]
    d_rnn = w_rnn_proj.shape[1]
    d_attn = w_attn_proj.shape[1]
    n_q = attn_sinks.shape[1]
    head_dim = d_attn // n_q
    d_kv = (w_in.shape[2] - 2 * d_rnn - 2 * d_attn - 2 * d) // 2
    n_kv = d_kv // head_dim
    assert bsz == 1

    xs = x.reshape(bsz * seq, d)
    bf = jnp.bfloat16
    for l in range(depth):
        mod = _modulation(c, w_ada[l], b_ada[l])
        shift, scale, gate = mod[:, :d], mod[:, d:2 * d], mod[:, 2 * d:]
        wl = w_in[l]
        c_u, c_g, c_q, c_kv = 0, d_rnn, 2 * d_rnn, 2 * d_rnn + d_attn
        c_ga = c_kv + 2 * d_kv
        c_mr, c_ma = c_ga + d_attn, c_ga + d_attn + d
        w_kv = _cast_cols(CastJob(wl, c_kv, 2 * d_kv))
        h, kv, w_u = _norm_proj(xs, norm_gain[l].reshape(1, d), scale, shift, w_kv,
                                CastJob(wl, c_u, d_rnn))
        proj = functools.partial(_matmul, h)
        u_rnn, w_g = proj(w_u, out_dtype=jnp.float32, cast=CastJob(wl, c_g, d_rnn),
                          name="proj_u")
        sg_rnn, w_q = proj(w_g, out_dtype=jnp.float32, epilogue=_ep_silu,
                           cast=CastJob(wl, c_q, d_attn), name="proj_g_rnn")
        q, w_ga = proj(w_q, out_dtype=bf, cast=CastJob(wl, c_ga, d_attn), name="proj_q")
        sg_attn, w_mr = proj(w_ga, out_dtype=jnp.float32, epilogue=_ep_silu,
                             cast=CastJob(wl, c_mr, d), name="proj_g_attn")
        sm_rnn, w_ma = proj(w_mr, out_dtype=jnp.float32, epilogue=_ep_sigmoid,
                            cast=CastJob(wl, c_ma, d), name="proj_m_rnn")
        sm_attn, w_rp = proj(w_ma, out_dtype=jnp.float32, epilogue=_ep_sigmoid,
                             cast=CastJob(w_rnn_proj[l], 0, d), name="proj_m_attn")

        y_rnn, w_ap = _rglru(u_rnn, sg_rnn, conv_w[l], conv_b[l], w_gate_a[l], b_gate_a[l],
                             w_gate_x[l], b_gate_x[l], lru_lambda[l],
                             CastJob(w_attn_proj[l], 0, d))
        y_attn, w_o = _attention(q, kv, sg_attn, attn_sinks[l], CastJob(w_out[l], 0, d),
                                 n_kv=n_kv, head_dim=head_dim)

        merged = _merge(y_rnn, y_attn, w_rp, w_ap, sm_rnn, sm_attn)
        if l + 1 < depth:
            xs = _matmul(merged, w_o, out_dtype=jnp.float32, epilogue=_ep_residual,
                         tiles=(xs,), rows=(gate,), name="out_proj")
        else:
            xs = _out_norm(merged, w_o, xs, gate, final_norm_gain.reshape(1, d))
    return xs.reshape(bsz, seq, d)
```
